```python
import math
import jax, jax.numpy as jnp
from jax import lax
import numpy as np


D_MODEL = 2048
BATCH = 1
SEQ = 8192
DEPTH = 2

CHUNK = 64
MIX_WIDTH = D_MODEL
EPS = 1e-6

HEAD_DIM_A = 64
N_HEADS_A = (MIX_WIDTH // 2) // HEAD_DIM_A
N_KV_A = 4
GROUP_A = N_HEADS_A // N_KV_A
WINDOW = 128
WINDOW_CHUNKS = WINDOW // CHUNK
Q_BLOCK = 128
HIST = WINDOW_CHUNKS * CHUNK
KEY_SPAN = Q_BLOCK + HIST

NUM_BUCKETS = 32
MAX_DISTANCE = 128

LRU_WIDTH = MIX_WIDTH // 2
LRU_BLOCKS = 16
LRU_BLOCK_DIM = LRU_WIDTH // LRU_BLOCKS
CONV_WIDTH = 4
LRU_C = 8.0

HEAD_DIM_C = 128
N_HEADS_C = MIX_WIDTH // HEAD_DIM_C
SB_BLOCK = 128

D_FF = 4 * D_MODEL

Q_COLS = N_HEADS_A * HEAD_DIM_A
KV_COLS = N_KV_A * HEAD_DIM_A
OFF_K = Q_COLS
OFF_V = OFF_K + KV_COLS
OFF_X = OFF_V + KV_COLS
OFF_G = OFF_X + LRU_WIDTH
EVEN_IN = OFF_G + LRU_WIDTH
ODD_IN = 3 * MIX_WIDTH
N_EVEN = (DEPTH + 1) // 2
N_ODD = DEPTH // 2

kernel_name = "chunk_causal_hybrid_swa_rglru_stickbreak"


def rms_norm(x, g):
    x32 = x.astype(jnp.float32)
    y = x32 * lax.rsqrt(jnp.mean(x32 * x32, axis=-1, keepdims=True) + EPS)
    return (y * g.astype(jnp.float32)).astype(x.dtype)


def t5_bucket(rel):
    half = NUM_BUCKETS // 2
    ret = jnp.where(rel > 0, half, 0)
    n = jnp.abs(rel)
    max_exact = half // 2
    nf = jnp.maximum(n, 1).astype(jnp.float32)
    large = max_exact + (jnp.log(nf / max_exact) / math.log(MAX_DISTANCE / max_exact)
                         * (half - max_exact)).astype(jnp.int32)
    large = jnp.minimum(large, half - 1)
    return ret + jnp.where(n < max_exact, n, large)


def swa_sink_attention(q, k, v, sinks, rel_bias):
    B, S = q.shape[0], q.shape[1]
    nb = S // Q_BLOCK
    qb = q.reshape(B, nb, Q_BLOCK, N_KV_A, GROUP_A, HEAD_DIM_A)
    kidx = jnp.arange(nb)[:, None] * Q_BLOCK + jnp.arange(KEY_SPAN)[None, :]
    kp = jnp.pad(k, ((0, 0), (HIST, 0), (0, 0), (0, 0)))
    vp = jnp.pad(v, ((0, 0), (HIST, 0), (0, 0), (0, 0)))
    kb = jnp.take(kp, kidx, axis=1)
    vb = jnp.take(vp, kidx, axis=1)
    logits = jnp.einsum('bnqhgd,bnshd->bnhgqs', qb, kb,
                        preferred_element_type=jnp.float32) * (HEAD_DIM_A ** -0.5)
    qi = jnp.arange(Q_BLOCK)
    sj = jnp.arange(KEY_SPAN)
    rel = (sj[None, :] - HIST) - qi[:, None]
    bias = rel_bias[t5_bucket(rel)].astype(jnp.float32)
    bias = bias.transpose(2, 0, 1).reshape(N_KV_A, GROUP_A, Q_BLOCK, KEY_SPAN)
    q_chunk = qi // CHUNK
    k_chunk = (sj - HIST) // CHUNK
    in_window = (k_chunk[None, :] <= q_chunk[:, None]) & \
                (k_chunk[None, :] >= q_chunk[:, None] - WINDOW_CHUNKS)
    valid = in_window[None] & ((kidx - HIST) >= 0)[:, None, :]
    logits = jnp.where(valid[None, :, None, None], logits + bias, -jnp.inf)
    sink = sinks.astype(jnp.float32).reshape(N_KV_A, GROUP_A)[None, None, :, :, None, None]
    m = jnp.maximum(jnp.max(logits, axis=-1, keepdims=True), sink)
    p = jnp.exp(logits - m)
    denom = jnp.sum(p, axis=-1, keepdims=True) + jnp.exp(sink - m)
    w = (p / denom).astype(v.dtype)
    out = jnp.einsum('bnhgqs,bnshd->bnqhgd', w, vb)
    return out.reshape(B, S, N_HEADS_A * HEAD_DIM_A)


def rg_lru_branch(xb, gb, conv_w, conv_b, gate_a_w, gate_a_b, gate_x_w, gate_x_b, lru_lambda):
    B, S, C = xb.shape
    xc = lax.conv_general_dilated(xb, conv_w[:, None, :], window_strides=(1,),
                                  padding=[(CONV_WIDTH - 1, 0)],
                                  dimension_numbers=('NWC', 'WIO', 'NWC'),
                                  feature_group_count=C) + conv_b
    xh = xc.reshape(B, S, LRU_BLOCKS, LRU_BLOCK_DIM)
    r = jax.nn.sigmoid((jnp.einsum('bshi,hij->bshj', xh, gate_a_w).reshape(B, S, C)
                        + gate_a_b).astype(jnp.float32))
    i = jax.nn.sigmoid((jnp.einsum('bshi,hij->bshj', xh, gate_x_w).reshape(B, S, C)
                        + gate_x_b).astype(jnp.float32))
    log_a = -LRU_C * r * jax.nn.softplus(-lru_lambda.astype(jnp.float32))
    a = jnp.exp(log_a)
    u = jnp.sqrt(-jnp.expm1(2.0 * log_a)) * (i * xc.astype(jnp.float32))

    def combine(left, right):
        a1, b1 = left
        a2, b2 = right
        return a1 * a2, a2 * b1 + b2

    _, h = lax.associative_scan(combine, (a, u), axis=1)
    y = jax.nn.gelu(gb.astype(jnp.float32), approximate=True) * h
    return y.astype(xb.dtype)


def stick_breaking_attention(q, k, v):
    B, S, H, D = q.shape
    nb = S // SB_BLOCK
    kpos = jnp.arange(S)
    qblocks = q.reshape(B, nb, SB_BLOCK, H, D).transpose(1, 0, 3, 2, 4)

    def one_block(args):
        bi, qb = args
        z = jnp.einsum('bhqd,bshd->bhqs', qb, k,
                       preferred_element_type=jnp.float32) * (HEAD_DIM_C ** -0.5)
        qpos = bi * SB_BLOCK + jnp.arange(SB_BLOCK)
        before = kpos[None, :] < qpos[:, None]
        log_keep = jnp.where(before, jax.nn.log_sigmoid(-z), 0.0)
        stick = lax.cumsum(log_keep, axis=3, reverse=True) - log_keep
        w = jnp.where(before, jnp.exp(jax.nn.log_sigmoid(z) + stick), 0.0)
        return jnp.einsum('bhqs,bshd->bqhd', w.astype(v.dtype), v)

    out = lax.map(one_block, (jnp.arange(nb), qblocks))
    return out.transpose(1, 0, 2, 3, 4).reshape(B, S, H * D)


def even_mixer(hn, w_in, conv_w, conv_b, gate_a_w, gate_a_b, gate_x_w, gate_x_b,
               lru_lambda, sinks, w_out, rel_bias):
    B, S, _ = hn.shape
    proj = hn @ w_in
    q = proj[..., :OFF_K].reshape(B, S, N_HEADS_A, HEAD_DIM_A)
    k = proj[..., OFF_K:OFF_V].reshape(B, S, N_KV_A, HEAD_DIM_A)
    v = proj[..., OFF_V:OFF_X].reshape(B, S, N_KV_A, HEAD_DIM_A)
    xb = proj[..., OFF_X:OFF_G]
    gb = proj[..., OFF_G:]
    y_a = swa_sink_attention(q, k, v, sinks, rel_bias)
    y_b = rg_lru_branch(xb, gb, conv_w, conv_b, gate_a_w, gate_a_b, gate_x_w, gate_x_b, lru_lambda)
    return jnp.concatenate([y_a, y_b], axis=-1) @ w_out


def odd_mixer(hn, w_in, w_out):
    B, S, _ = hn.shape
    proj = (hn @ w_in).reshape(B, S, 3, N_HEADS_C, HEAD_DIM_C)
    y = stick_breaking_attention(proj[:, :, 0], proj[:, :, 1], proj[:, :, 2])
    return y @ w_out


def squared_relu_mlp(hn, w_up, w_down):
    return jnp.square(jax.nn.relu(hn @ w_up)) @ w_down


def setup_inputs(seed: int = 0) -> dict:
    key = jax.random.key(seed)
    ks = jax.random.split(key, 20)
    f = jnp.float32

    def nrm(k, shape, scale):
        return jax.random.normal(k, shape, f) * scale

    x = nrm(ks[0], (BATCH, SEQ, D_MODEL), 1.0)
    rel_bias = nrm(ks[1], (NUM_BUCKETS, N_HEADS_A), 0.5)
    norm_mix = 1.0 + nrm(ks[2], (DEPTH, D_MODEL), 0.02)
    even_w_in = nrm(ks[3], (N_EVEN, D_MODEL, EVEN_IN), D_MODEL ** -0.5)
    even_conv_w = nrm(ks[4], (N_EVEN, CONV_WIDTH, LRU_WIDTH), CONV_WIDTH ** -0.5)
    even_conv_b = nrm(ks[5], (N_EVEN, LRU_WIDTH), 0.02)
    even_gate_a_w = nrm(ks[6], (N_EVEN, LRU_BLOCKS, LRU_BLOCK_DIM, LRU_BLOCK_DIM), LRU_BLOCK_DIM ** -0.5)
    even_gate_a_b = nrm(ks[7], (N_EVEN, LRU_WIDTH), 0.02)
    even_gate_x_w = nrm(ks[8], (N_EVEN, LRU_BLOCKS, LRU_BLOCK_DIM, LRU_BLOCK_DIM), LRU_BLOCK_DIM ** -0.5)
    even_gate_x_b = nrm(ks[9], (N_EVEN, LRU_WIDTH), 0.02)
    a_c = jax.random.uniform(ks[10], (N_EVEN, LRU_WIDTH), f, 0.9, 0.999)
    a0 = a_c ** (1.0 / LRU_C)
    even_lru_lambda = jnp.log(a0) - jnp.log1p(-a0)
    even_sinks = nrm(ks[11], (N_EVEN, N_HEADS_A), 0.5)
    even_w_out = nrm(ks[12], (N_EVEN, MIX_WIDTH, D_MODEL), MIX_WIDTH ** -0.5)
    odd_w_in = nrm(ks[13], (N_ODD, D_MODEL, ODD_IN), D_MODEL ** -0.5)
    odd_w_out = nrm(ks[14], (N_ODD, MIX_WIDTH, D_MODEL), MIX_WIDTH ** -0.5)
    norm_mlp = 1.0 + nrm(ks[15], (DEPTH, D_MODEL), 0.02)
    w_up = nrm(ks[16], (DEPTH, D_MODEL, D_FF), D_MODEL ** -0.5)
    w_down = nrm(ks[17], (DEPTH, D_FF, D_MODEL), D_FF ** -0.5)
    final_norm = 1.0 + nrm(ks[18], (D_MODEL,), 0.02)
    return {"x": x, "rel_bias": rel_bias, "norm_mix": norm_mix,
            "even_w_in": even_w_in, "even_conv_w": even_conv_w, "even_conv_b": even_conv_b,
            "even_gate_a_w": even_gate_a_w, "even_gate_a_b": even_gate_a_b,
            "even_gate_x_w": even_gate_x_w, "even_gate_x_b": even_gate_x_b,
            "even_lru_lambda": even_lru_lambda, "even_sinks": even_sinks,
            "even_w_out": even_w_out, "odd_w_in": odd_w_in, "odd_w_out": odd_w_out,
            "norm_mlp": norm_mlp, "w_up": w_up, "w_down": w_down, "final_norm": final_norm}


def reference(x, rel_bias, norm_mix, even_w_in, even_conv_w, even_conv_b,
              even_gate_a_w, even_gate_a_b, even_gate_x_w, even_gate_x_b,
              even_lru_lambda, even_sinks, even_w_out, odd_w_in, odd_w_out,
              norm_mlp, w_up, w_down, final_norm):
    h = x
    for layer in range(DEPTH):
        j = layer // 2
        hn = rms_norm(h, norm_mix[layer])
        if layer % 2 == 0:
            mix = even_mixer(hn, even_w_in[j], even_conv_w[j], even_conv_b[j],
                             even_gate_a_w[j], even_gate_a_b[j], even_gate_x_w[j],
                             even_gate_x_b[j], even_lru_lambda[j], even_sinks[j],
                             even_w_out[j], rel_bias)
        else:
            mix = odd_mixer(hn, odd_w_in[j], odd_w_out[j])
        h = h + mix
        h = h + squared_relu_mlp(rms_norm(h, norm_mlp[layer]), w_up[layer], w_down[layer])
    return rms_norm(h, final_norm)
```

```python
import functools
import math

import jax
import jax.numpy as jnp
from jax import lax
from jax.experimental import pallas as pl
from jax.experimental.pallas import tpu as pltpu

F32 = jnp.float32
BF16 = jnp.bfloat16

D_MODEL = 2048
SEQ = 8192
CHUNK = 64
EPS = 1e-6

HEAD_DIM_A = 64
N_HEADS_A = 16
N_KV_A = 4
GROUP_A = N_HEADS_A // N_KV_A
WINDOW_CHUNKS = 2
Q_BLOCK = 128
HIST = WINDOW_CHUNKS * CHUNK
KEY_SPAN = Q_BLOCK + HIST
NUM_BUCKETS = 32
MAX_DISTANCE = 128

LRU_WIDTH = 1024
LRU_BLOCKS = 16
LRU_BLOCK_DIM = LRU_WIDTH // LRU_BLOCKS
CONV_WIDTH = 4
LRU_C = 8.0

HEAD_DIM_C = 128
N_HEADS_C = 16
D_FF = 4 * D_MODEL

Q_COLS = N_HEADS_A * HEAD_DIM_A
KV_COLS = N_KV_A * HEAD_DIM_A

V7X_LANES = 128
V7X_SUBLANES = 8
V7X_VMEM_LIMIT_BYTES = 56 * 1024 * 1024

MASK_NEG = -1e30


def _params(n_grid_dims):
    return pltpu.CompilerParams(
        dimension_semantics=("arbitrary",) * n_grid_dims,
        vmem_limit_bytes=V7X_VMEM_LIMIT_BYTES)


def _rms_norm_f32(x, g):
    ms = jnp.mean(x * x, axis=-1, keepdims=True)
    return (x * lax.rsqrt(ms + EPS)) * g


def _dot(a, b):
    return jnp.dot(a, b, preferred_element_type=F32)


def _dot_nt(a, b):
    return lax.dot_general(a, b, (((1,), (1,)), ((), ())), preferred_element_type=F32)


def _norm_matmul_body(x_ref, g_ref, w_ref, o_ref, hn_ref):
    @pl.when(pl.program_id(1) == 0)
    def _():
        hn_ref[...] = _rms_norm_f32(x_ref[...], g_ref[...]).astype(BF16)

    o_ref[...] = _dot(hn_ref[...], w_ref[...]).astype(o_ref.dtype)


def _norm_matmul(x, g, w, out_dtype, tm, tn):
    m, k = x.shape
    n = w.shape[1]
    return pl.pallas_call(
        _norm_matmul_body,
        grid=(m // tm, n // tn),
        in_specs=[pl.BlockSpec((tm, k), lambda i, j: (i, 0)),
                  pl.BlockSpec((1, k), lambda i, j: (0, 0)),
                  pl.BlockSpec((k, tn), lambda i, j: (0, j))],
        out_specs=pl.BlockSpec((tm, tn), lambda i, j: (i, j)),
        out_shape=jax.ShapeDtypeStruct((m, n), out_dtype),
        scratch_shapes=[pltpu.VMEM((tm, k), BF16)],
        compiler_params=_params(2),
        name="norm_matmul",
    )(x, g.reshape(1, k), w)


def _matmul_res_body(*refs, n_pairs):
    a_refs = refs[:n_pairs]
    w_refs = refs[n_pairs:2 * n_pairs]
    res_ref, o_ref = refs[2 * n_pairs], refs[2 * n_pairs + 1]
    acc = res_ref[...]
    for a_ref, w_ref in zip(a_refs, w_refs):
        acc = acc + _dot(a_ref[...], w_ref[...])
    o_ref[...] = acc


def _matmul_res(a_list, w_list, res, tm, tn):
    m, n = res.shape
    n_pairs = len(a_list)
    in_specs = [pl.BlockSpec((tm, a.shape[1]), lambda i, j: (i, 0)) for a in a_list]
    in_specs += [pl.BlockSpec((w.shape[0], tn), lambda i, j: (0, j)) for w in w_list]
    in_specs += [pl.BlockSpec((tm, tn), lambda i, j: (i, j))]
    return pl.pallas_call(
        functools.partial(_matmul_res_body, n_pairs=n_pairs),
        grid=(m // tm, n // tn),
        in_specs=in_specs,
        out_specs=pl.BlockSpec((tm, tn), lambda i, j: (i, j)),
        out_shape=jax.ShapeDtypeStruct((m, n), F32),
        compiler_params=_params(2),
        name="matmul_res",
    )(*a_list, *w_list, res)


def _mlp_body(*refs, final_norm):
    if final_norm:
        h_ref, g_ref, wu_ref, wd_ref, fg_ref, o_ref, hn_ref, acc_ref = refs
    else:
        h_ref, g_ref, wu_ref, wd_ref, o_ref, hn_ref, acc_ref = refs
    f = pl.program_id(1)

    @pl.when(f == 0)
    def _():
        hn_ref[...] = _rms_norm_f32(h_ref[...], g_ref[...]).astype(BF16)

    up = _dot(hn_ref[...], wu_ref[...])
    act = jnp.square(jnp.maximum(up, 0.0)).astype(BF16)
    contrib = _dot(act, wd_ref[...])

    @pl.when(f == 0)
    def _():
        acc_ref[...] = contrib

    @pl.when(f > 0)
    def _():
        acc_ref[...] += contrib

    @pl.when(f == pl.num_programs(1) - 1)
    def _():
        out = h_ref[...] + acc_ref[...]
        if final_norm:
            out = _rms_norm_f32(out, fg_ref[...])
        o_ref[...] = out


def _mlp(h, g, w_up, w_down, final_g, tm, tf):
    m, d = h.shape
    ff = w_up.shape[1]
    final_norm = final_g is not None
    in_specs = [pl.BlockSpec((tm, d), lambda i, f: (i, 0)),
                pl.BlockSpec((1, d), lambda i, f: (0, 0)),
                pl.BlockSpec((d, tf), lambda i, f: (0, f)),
                pl.BlockSpec((tf, d), lambda i, f: (f, 0))]
    args = [h, g.reshape(1, d), w_up, w_down]
    if final_norm:
        in_specs.append(pl.BlockSpec((1, d), lambda i, f: (0, 0)))
        args.append(final_g.reshape(1, d))
    return pl.pallas_call(
        functools.partial(_mlp_body, final_norm=final_norm),
        grid=(m // tm, ff // tf),
        in_specs=in_specs,
        out_specs=pl.BlockSpec((tm, d), lambda i, f: (i, 0)),
        out_shape=jax.ShapeDtypeStruct((m, d), F32),
        scratch_shapes=[pltpu.VMEM((tm, d), BF16), pltpu.VMEM((tm, d), F32)],
        compiler_params=_params(2),
        name="mlp",
    )(*args)


SWA_K_BLOCK = (Q_COLS + 2 * LRU_WIDTH) // KV_COLS
SWA_V_BLOCK = SWA_K_BLOCK + 1


def _swa_body(relb_ref, sink_ref, bucket_ref, q_ref, kc_ref, kp_ref, vc_ref, vp_ref,
              o_ref, tab_ref):
    i = pl.program_id(0)
    rows = GROUP_A * Q_BLOCK

    @pl.when(i == 0)
    def _():
        bucket = bucket_ref[...]
        for head in range(N_HEADS_A):
            t = jnp.full((Q_BLOCK, KEY_SPAN), MASK_NEG, F32)
            for b in range(NUM_BUCKETS):
                t = jnp.where(bucket == b, relb_ref[b, head], t)
            g = head % GROUP_A
            tab_ref[head // GROUP_A, g * Q_BLOCK:(g + 1) * Q_BLOCK, :] = t

    hist_pen = jnp.where(i == 0, MASK_NEG, 0.0).astype(F32)

    for h in range(N_KV_A):
        cs = slice(h * HEAD_DIM_A, (h + 1) * HEAD_DIM_A)
        kc = kc_ref[:, cs].astype(BF16)
        kp = kp_ref[:, cs].astype(BF16)
        vc = vc_ref[:, cs].astype(BF16)
        vp = vp_ref[:, cs].astype(BF16)
        heads = [h * GROUP_A + g for g in range(GROUP_A)]
        qs = jnp.concatenate(
            [q_ref[:, hd * HEAD_DIM_A:(hd + 1) * HEAD_DIM_A] for hd in heads], axis=0)
        qs = (qs * (HEAD_DIM_A ** -0.5)).astype(BF16)
        lp = _dot_nt(qs, kp) + tab_ref[h, :, 0:HIST] + hist_pen
        lc = _dot_nt(qs, kc) + tab_ref[h, :, HIST:KEY_SPAN]
        sink = jnp.concatenate(
            [jnp.full((Q_BLOCK, 1), sink_ref[hd], F32) for hd in heads], axis=0)
        m = jnp.maximum(jnp.maximum(jnp.max(lp, axis=1, keepdims=True),
                                    jnp.max(lc, axis=1, keepdims=True)), sink)
        pp = jnp.exp(lp - m)
        pc = jnp.exp(lc - m)
        denom = (jnp.sum(pp, axis=1, keepdims=True) + jnp.sum(pc, axis=1, keepdims=True)
                 + jnp.exp(sink - m))
        acc = _dot(pp.astype(BF16), vp) + _dot(pc.astype(BF16), vc)
        out = acc / denom
        for g, hd in enumerate(heads):
            o_ref[:, hd * HEAD_DIM_A:(hd + 1) * HEAD_DIM_A] = (
                out[g * Q_BLOCK:(g + 1) * Q_BLOCK, :].astype(o_ref.dtype))
    del rows


def _swa(proj0, rel_bias, sinks, bucket):
    s = proj0.shape[0]
    nb = s // Q_BLOCK
    kv_spec = lambda blk, prev: pl.BlockSpec(
        (Q_BLOCK, KV_COLS),
        (lambda i: (jnp.maximum(i - 1, 0), blk)) if prev else (lambda i: (i, blk)))
    return pl.pallas_call(
        _swa_body,
        grid=(nb,),
        in_specs=[pl.BlockSpec(memory_space=pltpu.SMEM),
                  pl.BlockSpec(memory_space=pltpu.SMEM),
                  pl.BlockSpec((Q_BLOCK, KEY_SPAN), lambda i: (0, 0)),
                  pl.BlockSpec((Q_BLOCK, Q_COLS), lambda i: (i, 0)),
                  kv_spec(SWA_K_BLOCK, False), kv_spec(SWA_K_BLOCK, True),
                  kv_spec(SWA_V_BLOCK, False), kv_spec(SWA_V_BLOCK, True)],
        out_specs=pl.BlockSpec((Q_BLOCK, Q_COLS), lambda i: (i, 0)),
        out_shape=jax.ShapeDtypeStruct((s, Q_COLS), BF16),
        scratch_shapes=[pltpu.VMEM((N_KV_A, GROUP_A * Q_BLOCK, KEY_SPAN), F32)],
        compiler_params=_params(1),
        name="swa_attention",
    )(rel_bias, sinks, bucket, proj0, proj0, proj0, proj0, proj0)


def _t5_bucket(rel):
    half = NUM_BUCKETS // 2
    ret = jnp.where(rel > 0, half, 0)
    n = jnp.abs(rel)
    max_exact = half // 2
    nf = jnp.maximum(n, 1).astype(F32)
    large = max_exact + (jnp.log(nf / max_exact) / math.log(MAX_DISTANCE / max_exact)
                         * (half - max_exact)).astype(jnp.int32)
    large = jnp.minimum(large, half - 1)
    return ret + jnp.where(n < max_exact, n, large)


def _swa_bucket_table():
    qi = jnp.arange(Q_BLOCK)
    sj = jnp.arange(KEY_SPAN)
    rel = (sj[None, :] - HIST) - qi[:, None]
    q_chunk = qi // CHUNK
    k_chunk = (sj - HIST) // CHUNK
    in_window = ((k_chunk[None, :] <= q_chunk[:, None])
                 & (k_chunk[None, :] >= q_chunk[:, None] - WINDOW_CHUNKS))
    return jnp.where(in_window, _t5_bucket(rel), -1).astype(jnp.int32)


LRU_TIME_BLOCK = 256
LRU_PAD = V7X_SUBLANES
LRU_LANE_TILES = LRU_WIDTH // V7X_LANES


def _neg_expm1(y):
    e = jnp.exp(y)
    safe_log = jnp.where(e == 1.0, 1.0, jnp.log(e))
    near_zero = jnp.where(e == 1.0, -y, (1.0 - e) * y / safe_log)
    return jnp.where(y > -0.5, near_zero, 1.0 - e)


def _lru_body(xb_ref, gb_ref, cw_ref, cb_ref, wa_ref, ba_ref, wx_ref, bx_ref, lam_ref,
              o_ref, xpad_ref, a_ref, u_ref, hs_ref, hc_ref):
    t = pl.program_id(0)
    tb = LRU_TIME_BLOCK

    @pl.when(t == 0)
    def _():
        xpad_ref[0:LRU_PAD, :] = jnp.zeros((LRU_PAD, LRU_WIDTH), F32)
        hc_ref[...] = jnp.zeros((1, LRU_WIDTH), F32)

    x = xb_ref[...]
    xpad_ref[LRU_PAD:LRU_PAD + tb, :] = x
    xc = cb_ref[...] + cw_ref[CONV_WIDTH - 1:CONV_WIDTH, :] * x
    for w in range(CONV_WIDTH - 1):
        off = LRU_PAD - (CONV_WIDTH - 1) + w
        xc = xc + cw_ref[w:w + 1, :] * xpad_ref[off:off + tb, :]
    xpad_ref[0:LRU_PAD, :] = x[tb - LRU_PAD:tb, :]

    neg_c_softplus = -LRU_C * jax.nn.softplus(-lam_ref[...])
    for c in range(LRU_LANE_TILES):
        cs = slice(c * V7X_LANES, (c + 1) * V7X_LANES)
        xcc = xc[:, cs]
        xcb = xcc.astype(BF16)
        r = jax.nn.sigmoid(_dot(xcb, wa_ref[c]) + ba_ref[:, cs])
        ig = jax.nn.sigmoid(_dot(xcb, wx_ref[c]) + bx_ref[:, cs])
        log_a = r * neg_c_softplus[:, cs]
        a_ref[:, cs] = jnp.exp(log_a)
        u_ref[:, cs] = jnp.sqrt(_neg_expm1(2.0 * log_a)) * (ig * xcc)

    def scan_rows(r8, h):
        base = pl.multiple_of(r8 * V7X_SUBLANES, V7X_SUBLANES)
        for r in range(V7X_SUBLANES):
            h = a_ref[pl.ds(base + r, 1), :] * h + u_ref[pl.ds(base + r, 1), :]
            hs_ref[pl.ds(base + r, 1), :] = h
        return h

    hc_ref[...] = lax.fori_loop(0, tb // V7X_SUBLANES, scan_rows, hc_ref[...])
    o_ref[...] = (jax.nn.gelu(gb_ref[...], approximate=True) * hs_ref[...]).astype(o_ref.dtype)


def _lru(proj0, conv_w, conv_b, wa_pairs, ba, wx_pairs, bx, lam):
    s = proj0.shape[0]
    tb = LRU_TIME_BLOCK
    c = LRU_WIDTH
    row = lambda v: v.reshape(1, c)
    full2 = lambda shape: pl.BlockSpec(shape, lambda t: (0,) * len(shape))
    return pl.pallas_call(
        _lru_body,
        grid=(s // tb,),
        in_specs=[pl.BlockSpec((tb, c), lambda t: (t, 1)),
                  pl.BlockSpec((tb, c), lambda t: (t, 2)),
                  full2((CONV_WIDTH, c)), full2((1, c)),
                  full2(wa_pairs.shape), full2((1, c)),
                  full2(wx_pairs.shape), full2((1, c)), full2((1, c))],
        out_specs=pl.BlockSpec((tb, c), lambda t: (t, 0)),
        out_shape=jax.ShapeDtypeStruct((s, c), BF16),
        scratch_shapes=[pltpu.VMEM((tb + LRU_PAD, c), F32),
                        pltpu.VMEM((tb, c), F32), pltpu.VMEM((tb, c), F32),
                        pltpu.VMEM((tb, c), F32), pltpu.VMEM((1, c), F32)],
        compiler_params=_params(1),
        name="rg_lru",
    )(proj0, proj0, conv_w, row(conv_b), wa_pairs, row(ba), wx_pairs, row(bx), row(lam))


def _pair_block_diag(w):
    nb, bd, _ = w.shape
    wp = w.reshape(nb // 2, 2, bd, bd)
    z = jnp.zeros((nb // 2, bd, bd), w.dtype)
    top = jnp.concatenate([wp[:, 0], z], axis=2)
    bot = jnp.concatenate([z, wp[:, 1]], axis=2)
    return jnp.concatenate([top, bot], axis=1)


SB_TQ = 256
SB_TK = 256


def _sb_body(q_ref, k_ref, v_ref, u_ref, o_ref):
    qi = pl.program_id(1)
    q = q_ref[...]
    tri = u_ref[...]
    scale = HEAD_DIM_C ** -0.5
    row_id = lax.broadcasted_iota(jnp.int32, (SB_TQ, SB_TK), 0)
    col_id = lax.broadcasted_iota(jnp.int32, (SB_TQ, SB_TK), 1)
    before = col_id < row_id

    def block(j, carry, acc, diag):
        start = pl.multiple_of(j * SB_TK, SB_TK)
        kb = k_ref[pl.ds(start, SB_TK), :]
        vb = v_ref[pl.ds(start, SB_TK), :]
        z = _dot_nt(q, kb) * scale
        sp = jnp.maximum(z, 0.0) + jnp.log1p(jnp.exp(-jnp.abs(z)))
        lk = -sp
        if diag:
            lk = jnp.where(before, lk, 0.0)
        hi = lk.astype(BF16)
        lo = (lk - hi.astype(F32)).astype(BF16)
        suffix = _dot(hi, tri) + _dot(lo, tri)
        w = jnp.exp((z - sp) + (suffix + carry))
        if diag:
            w = jnp.where(before, w, 0.0)
        acc = acc + _dot(w.astype(BF16), vb)
        carry = carry + jnp.sum(lk, axis=1, keepdims=True)
        return carry, acc

    carry0 = jnp.zeros((SB_TQ, 1), F32)
    acc0 = jnp.zeros((SB_TQ, HEAD_DIM_C), F32)
    carry, acc = block(qi, carry0, acc0, True)

    def body(t, state):
        return block(qi - 1 - t, state[0], state[1], False)

    carry, acc = lax.fori_loop(0, qi, body, (carry, acc))
    o_ref[...] = acc.astype(o_ref.dtype)


def _stick_breaking(proj1):
    s = proj1.shape[0]
    hd = HEAD_DIM_C
    jj = jnp.arange(SB_TK)
    tri = (jj[:, None] > jj[None, :]).astype(BF16)
    return pl.pallas_call(
        _sb_body,
        grid=(N_HEADS_C, s // SB_TQ),
        in_specs=[pl.BlockSpec((SB_TQ, hd), lambda h, i: (i, h)),
                  pl.BlockSpec((s, hd), lambda h, i: (0, N_HEADS_C + h)),
                  pl.BlockSpec((s, hd), lambda h, i: (0, 2 * N_HEADS_C + h)),
                  pl.BlockSpec((SB_TK, SB_TK), lambda h, i: (0, 0))],
        out_specs=pl.BlockSpec((SB_TQ, hd), lambda h, i: (i, h)),
        out_shape=jax.ShapeDtypeStruct((s, N_HEADS_C * hd), BF16),
        compiler_params=_params(2),
        name="stick_breaking",
    )(proj1, proj1, proj1, tri)


def kernel(x, rel_bias, norm_mix, even_w_in, even_conv_w, even_conv_b, even_gate_a_w,
           even_gate_a_b, even_gate_x_w, even_gate_x_b, even_lru_lambda, even_sinks,
           even_w_out, odd_w_in, odd_w_out, norm_mlp, w_up, w_down, final_norm):
    b, s, d = x.shape
    h = x.reshape(b * s, d)

    w_in0 = even_w_in[0]
    off_k, off_v, off_x, off_g = Q_COLS, Q_COLS + KV_COLS, Q_COLS + 2 * KV_COLS, Q_COLS + 2 * KV_COLS + LRU_WIDTH
    w_in0 = jnp.concatenate([w_in0[:, :off_k], w_in0[:, off_x:off_g], w_in0[:, off_g:],
                             w_in0[:, off_k:off_v], w_in0[:, off_v:off_x]], axis=1).astype(BF16)
    proj0 = _norm_matmul(h, norm_mix[0], w_in0, F32, tm=1024, tn=512)
    y_a = _swa(proj0, rel_bias, even_sinks[0], _swa_bucket_table())
    y_b = _lru(proj0, even_conv_w[0], even_conv_b[0],
               _pair_block_diag(even_gate_a_w[0]).astype(BF16), even_gate_a_b[0],
               _pair_block_diag(even_gate_x_w[0]).astype(BF16), even_gate_x_b[0],
               even_lru_lambda[0])
    w_out0 = even_w_out[0].astype(BF16)
    h = _matmul_res([y_a, y_b], [w_out0[:Q_COLS], w_out0[Q_COLS:]], h, tm=1024, tn=512)
    h = _mlp(h, norm_mlp[0], w_up[0].astype(BF16), w_down[0].astype(BF16), None, tm=512, tf=512)

    proj1 = _norm_matmul(h, norm_mix[1], odd_w_in[0].astype(BF16), BF16, tm=1024, tn=512)
    y = _stick_breaking(proj1)
    h = _matmul_res([y], [odd_w_out[0].astype(BF16)], h, tm=1024, tn=512)
    h = _mlp(h, norm_mlp[1], w_up[1].astype(BF16), w_down[1].astype(BF16), final_norm,
             tm=512, tf=512)
    return h.reshape(b, s, d)
```

```python
import functools
import math

import jax
import jax.numpy as jnp
from jax import lax
from jax.experimental import pallas as pl
from jax.experimental.pallas import tpu as pltpu

F32 = jnp.float32
BF16 = jnp.bfloat16

D_MODEL = 2048
SEQ = 8192
CHUNK = 64
EPS = 1e-6

HEAD_DIM_A = 64
N_HEADS_A = 16
N_KV_A = 4
GROUP_A = N_HEADS_A // N_KV_A
WINDOW_CHUNKS = 2
Q_BLOCK = 128
HIST = WINDOW_CHUNKS * CHUNK
KEY_SPAN = Q_BLOCK + HIST
NUM_BUCKETS = 32
MAX_DISTANCE = 128

LRU_WIDTH = 1024
LRU_BLOCKS = 16
LRU_BLOCK_DIM = LRU_WIDTH // LRU_BLOCKS
CONV_WIDTH = 4
LRU_C = 8.0

HEAD_DIM_C = 128
N_HEADS_C = 16
D_FF = 4 * D_MODEL

Q_COLS = N_HEADS_A * HEAD_DIM_A
KV_COLS = N_KV_A * HEAD_DIM_A

V7X_LANES = 128
V7X_SUBLANES = 8
V7X_VMEM_LIMIT_BYTES = 56 * 1024 * 1024

MASK_NEG = -1e30


def _params(n_grid_dims):
    return pltpu.CompilerParams(
        dimension_semantics=("arbitrary",) * n_grid_dims,
        vmem_limit_bytes=V7X_VMEM_LIMIT_BYTES)


def _rms_norm_f32(x, g):
    ms = jnp.mean(x * x, axis=-1, keepdims=True)
    return (x * lax.rsqrt(ms + EPS)) * g


def _dot(a, b):
    return jnp.dot(a, b, preferred_element_type=F32)


def _dot_nt(a, b):
    return lax.dot_general(a, b, (((1,), (1,)), ((), ())), preferred_element_type=F32)


def _norm_matmul_body(x_ref, g_ref, w_ref, o_ref, hn_ref):
    @pl.when(pl.program_id(1) == 0)
    def _():
        hn_ref[...] = _rms_norm_f32(x_ref[...], g_ref[...]).astype(BF16)

    o_ref[...] = _dot(hn_ref[...], w_ref[...]).astype(o_ref.dtype)


def _norm_matmul(x, g, w, out_dtype, tm, tn):
    m, k = x.shape
    n = w.shape[1]
    return pl.pallas_call(
        _norm_matmul_body,
        grid=(m // tm, n // tn),
        in_specs=[pl.BlockSpec((tm, k), lambda i, j: (i, 0)),
                  pl.BlockSpec((1, k), lambda i, j: (0, 0)),
                  pl.BlockSpec((k, tn), lambda i, j: (0, j))],
        out_specs=pl.BlockSpec((tm, tn), lambda i, j: (i, j)),
        out_shape=jax.ShapeDtypeStruct((m, n), out_dtype),
        scratch_shapes=[pltpu.VMEM((tm, k), BF16)],
        compiler_params=_params(2),
        name="norm_matmul",
    )(x, g.reshape(1, k), w)


def _matmul_res_body(*refs, n_pairs):
    a_refs = refs[:n_pairs]
    w_refs = refs[n_pairs:2 * n_pairs]
    res_ref, o_ref = refs[2 * n_pairs], refs[2 * n_pairs + 1]
    acc = res_ref[...]
    for a_ref, w_ref in zip(a_refs, w_refs):
        acc = acc + _dot(a_ref[...], w_ref[...])
    o_ref[...] = acc


def _matmul_res(a_list, w_list, res, tm, tn):
    m, n = res.shape
    n_pairs = len(a_list)
    in_specs = [pl.BlockSpec((tm, a.shape[1]), lambda i, j: (i, 0)) for a in a_list]
    in_specs += [pl.BlockSpec((w.shape[0], tn), lambda i, j: (0, j)) for w in w_list]
    in_specs += [pl.BlockSpec((tm, tn), lambda i, j: (i, j))]
    return pl.pallas_call(
        functools.partial(_matmul_res_body, n_pairs=n_pairs),
        grid=(m // tm, n // tn),
        in_specs=in_specs,
        out_specs=pl.BlockSpec((tm, tn), lambda i, j: (i, j)),
        out_shape=jax.ShapeDtypeStruct((m, n), F32),
        compiler_params=_params(2),
        name="matmul_res",
    )(*a_list, *w_list, res)


def _mlp_body(*refs, final_norm):
    if final_norm:
        h_ref, g_ref, wu_ref, wd_ref, fg_ref, o_ref, hn_ref, acc_ref = refs
    else:
        h_ref, g_ref, wu_ref, wd_ref, o_ref, hn_ref, acc_ref = refs
    f = pl.program_id(1)

    @pl.when(f == 0)
    def _():
        hn_ref[...] = _rms_norm_f32(h_ref[...], g_ref[...]).astype(BF16)

    up = _dot(hn_ref[...], wu_ref[...])
    act = jnp.square(jnp.maximum(up, 0.0)).astype(BF16)
    contrib = _dot(act, wd_ref[...])

    @pl.when(f == 0)
    def _():
        acc_ref[...] = contrib

    @pl.when(f > 0)
    def _():
        acc_ref[...] += contrib

    @pl.when(f == pl.num_programs(1) - 1)
    def _():
        out = h_ref[...] + acc_ref[...]
        if final_norm:
            out = _rms_norm_f32(out, fg_ref[...])
        o_ref[...] = out


def _mlp(h, g, w_up, w_down, final_g, tm, tf):
    m, d = h.shape
    ff = w_up.shape[1]
    final_norm = final_g is not None
    in_specs = [pl.BlockSpec((tm, d), lambda i, f: (i, 0)),
                pl.BlockSpec((1, d), lambda i, f: (0, 0)),
                pl.BlockSpec((d, tf), lambda i, f: (0, f)),
                pl.BlockSpec((tf, d), lambda i, f: (f, 0))]
    args = [h, g.reshape(1, d), w_up, w_down]
    if final_norm:
        in_specs.append(pl.BlockSpec((1, d), lambda i, f: (0, 0)))
        args.append(final_g.reshape(1, d))
    return pl.pallas_call(
        functools.partial(_mlp_body, final_norm=final_norm),
        grid=(m // tm, ff // tf),
        in_specs=in_specs,
        out_specs=pl.BlockSpec((tm, d), lambda i, f: (i, 0)),
        out_shape=jax.ShapeDtypeStruct((m, d), F32),
        scratch_shapes=[pltpu.VMEM((tm, d), BF16), pltpu.VMEM((tm, d), F32)],
        compiler_params=_params(2),
        name="mlp",
    )(*args)


SWA_K_BLOCK = (Q_COLS + 2 * LRU_WIDTH) // KV_COLS
SWA_V_BLOCK = SWA_K_BLOCK + 1


def _swa_body(relb_ref, sink_ref, bucket_ref, q_ref, kc_ref, kp_ref, vc_ref, vp_ref,
              o_ref, tab_ref):
    i = pl.program_id(0)
    rows = GROUP_A * Q_BLOCK

    @pl.when(i == 0)
    def _():
        bucket = bucket_ref[...]
        for head in range(N_HEADS_A):
            t = jnp.full((Q_BLOCK, KEY_SPAN), MASK_NEG, F32)
            for b in range(NUM_BUCKETS):
                t = jnp.where(bucket == b, relb_ref[b, head], t)
            g = head % GROUP_A
            tab_ref[head // GROUP_A, g * Q_BLOCK:(g + 1) * Q_BLOCK, :] = t

    hist_pen = jnp.where(i == 0, MASK_NEG, 0.0).astype(F32)

    for h in range(N_KV_A):
        cs = slice(h * HEAD_DIM_A, (h + 1) * HEAD_DIM_A)
        kc = kc_ref[:, cs].astype(BF16)
        kp = kp_ref[:, cs].astype(BF16)
        vc = vc_ref[:, cs].astype(BF16)
        vp = vp_ref[:, cs].astype(BF16)
        heads = [h * GROUP_A + g for g in range(GROUP_A)]
        qs = jnp.concatenate(
            [q_ref[:, hd * HEAD_DIM_A:(hd + 1) * HEAD_DIM_A] for hd in heads], axis=0)
        qs = (qs * (HEAD_DIM_A ** -0.5)).astype(BF16)
        lp = _dot_nt(qs, kp) + tab_ref[h, :, 0:HIST] + hist_pen
        lc = _dot_nt(qs, kc) + tab_ref[h, :, HIST:KEY_SPAN]
        sink = jnp.concatenate(
            [jnp.full((Q_BLOCK, 1), sink_ref[hd], F32) for hd in heads], axis=0)
        m = jnp.maximum(jnp.maximum(jnp.max(lp, axis=1, keepdims=True),
                                    jnp.max(lc, axis=1, keepdims=True)), sink)
        pp = jnp.exp(lp - m)
        pc = jnp.exp(lc - m)
        denom = (jnp.sum(pp, axis=1, keepdims=True) + jnp.sum(pc, axis=1, keepdims=True)
                 + jnp.exp(sink - m))
        acc = _dot(pp.astype(BF16), vp) + _dot(pc.astype(BF16), vc)
        out = acc / denom
        for g, hd in enumerate(heads):
            o_ref[:, hd * HEAD_DIM_A:(hd + 1) * HEAD_DIM_A] = (
                out[g * Q_BLOCK:(g + 1) * Q_BLOCK, :].astype(o_ref.dtype))
    del rows


def _swa(proj0, rel_bias, sinks, bucket):
    s = proj0.shape[0]
    nb = s // Q_BLOCK
    kv_spec = lambda blk, prev: pl.BlockSpec(
        (Q_BLOCK, KV_COLS),
        (lambda i: (jnp.maximum(i - 1, 0), blk)) if prev else (lambda i: (i, blk)))
    return pl.pallas_call(
        _swa_body,
        grid=(nb,),
        in_specs=[pl.BlockSpec(memory_space=pltpu.SMEM),
                  pl.BlockSpec(memory_space=pltpu.SMEM),
                  pl.BlockSpec((Q_BLOCK, KEY_SPAN), lambda i: (0, 0)),
                  pl.BlockSpec((Q_BLOCK, Q_COLS), lambda i: (i, 0)),
                  kv_spec(SWA_K_BLOCK, False), kv_spec(SWA_K_BLOCK, True),
                  kv_spec(SWA_V_BLOCK, False), kv_spec(SWA_V_BLOCK, True)],
        out_specs=pl.BlockSpec((Q_BLOCK, Q_COLS), lambda i: (i, 0)),
        out_shape=jax.ShapeDtypeStruct((s, Q_COLS), BF16),
        scratch_shapes=[pltpu.VMEM((N_KV_A, GROUP_A * Q_BLOCK, KEY_SPAN), F32)],
        compiler_params=_params(1),
        name="swa_attention",
    )(rel_bias, sinks, bucket, proj0, proj0, proj0, proj0, proj0)


def _t5_bucket(rel):
    half = NUM_BUCKETS // 2
    ret = jnp.where(rel > 0, half, 0)
    n = jnp.abs(rel)
    max_exact = half // 2
    nf = jnp.maximum(n, 1).astype(F32)
    large = max_exact + (jnp.log(nf / max_exact) / math.log(MAX_DISTANCE / max_exact)
                         * (half - max_exact)).astype(jnp.int32)
    large = jnp.minimum(large, half - 1)
    return ret + jnp.where(n < max_exact, n, large)


def _swa_bucket_table():
    qi = jnp.arange(Q_BLOCK)
    sj = jnp.arange(KEY_SPAN)
    rel = (sj[None, :] - HIST) - qi[:, None]
    q_chunk = qi // CHUNK
    k_chunk = (sj - HIST) // CHUNK
    in_window = ((k_chunk[None, :] <= q_chunk[:, None])
                 & (k_chunk[None, :] >= q_chunk[:, None] - WINDOW_CHUNKS))
    return jnp.where(in_window, _t5_bucket(rel), -1).astype(jnp.int32)


LRU_TIME_BLOCK = 256
LRU_PAD = V7X_SUBLANES
LRU_LANE_TILES = LRU_WIDTH // V7X_LANES


def _neg_expm1(y):
    e = jnp.exp(y)
    safe_log = jnp.where(e == 1.0, 1.0, jnp.log(e))
    near_zero = jnp.where(e == 1.0, -y, (1.0 - e) * y / safe_log)
    return jnp.where(y > -0.5, near_zero, 1.0 - e)


def _lru_body(xb_ref, gb_ref, cw_ref, cb_ref, wa_ref, ba_ref, wx_ref, bx_ref, lam_ref,
              o_ref, xpad_ref, a_ref, u_ref, hs_ref, hc_ref):
    t = pl.program_id(0)
    tb = LRU_TIME_BLOCK

    @pl.when(t == 0)
    def _():
        xpad_ref[0:LRU_PAD, :] = jnp.zeros((LRU_PAD, LRU_WIDTH), F32)
        hc_ref[...] = jnp.zeros((1, LRU_WIDTH), F32)

    x = xb_ref[...]
    xpad_ref[LRU_PAD:LRU_PAD + tb, :] = x
    xc = cb_ref[...] + cw_ref[CONV_WIDTH - 1:CONV_WIDTH, :] * x
    for w in range(CONV_WIDTH - 1):
        off = LRU_PAD - (CONV_WIDTH - 1) + w
        xc = xc + cw_ref[w:w + 1, :] * xpad_ref[off:off + tb, :]
    xpad_ref[0:LRU_PAD, :] = x[tb - LRU_PAD:tb, :]

    neg_c_softplus = -LRU_C * jax.nn.softplus(-lam_ref[...])
    for c in range(LRU_LANE_TILES):
        cs = slice(c * V7X_LANES, (c + 1) * V7X_LANES)
        xcc = xc[:, cs]
        xcb = xcc.astype(BF16)
        r = jax.nn.sigmoid(_dot(xcb, wa_ref[c]) + ba_ref[:, cs])
        ig = jax.nn.sigmoid(_dot(xcb, wx_ref[c]) + bx_ref[:, cs])
        log_a = r * neg_c_softplus[:, cs]
        a_ref[:, cs] = jnp.exp(log_a)
        u_ref[:, cs] = jnp.sqrt(_neg_expm1(2.0 * log_a)) * (ig * xcc)

    def scan_rows(r8, h):
        base = pl.multiple_of(r8 * V7X_SUBLANES, V7X_SUBLANES)
        for r in range(V7X_SUBLANES):
            h = a_ref[pl.ds(base + r, 1), :] * h + u_ref[pl.ds(base + r, 1), :]
            hs_ref[pl.ds(base + r, 1), :] = h
        return h

    hc_ref[...] = lax.fori_loop(0, tb // V7X_SUBLANES, scan_rows, hc_ref[...])
    o_ref[...] = (jax.nn.gelu(gb_ref[...], approximate=True) * hs_ref[...]).astype(o_ref.dtype)


def _lru(proj0, conv_w, conv_b, wa_pairs, ba, wx_pairs, bx, lam):
    s = proj0.shape[0]
    tb = LRU_TIME_BLOCK
    c = LRU_WIDTH
    row = lambda v: v.reshape(1, c)
    full2 = lambda shape: pl.BlockSpec(shape, lambda t: (0,) * len(shape))
    return pl.pallas_call(
        _lru_body,
        grid=(s // tb,),
        in_specs=[pl.BlockSpec((tb, c), lambda t: (t, 1)),
                  pl.BlockSpec((tb, c), lambda t: (t, 2)),
                  full2((CONV_WIDTH, c)), full2((1, c)),
                  full2(wa_pairs.shape), full2((1, c)),
                  full2(wx_pairs.shape), full2((1, c)), full2((1, c))],
        out_specs=pl.BlockSpec((tb, c), lambda t: (t, 0)),
        out_shape=jax.ShapeDtypeStruct((s, c), BF16),
        scratch_shapes=[pltpu.VMEM((tb + LRU_PAD, c), F32),
                        pltpu.VMEM((tb, c), F32), pltpu.VMEM((tb, c), F32),
                        pltpu.VMEM((tb, c), F32), pltpu.VMEM((1, c), F32)],
        compiler_params=_params(1),
        name="rg_lru",
    )(proj0, proj0, conv_w, row(conv_b), wa_pairs, row(ba), wx_pairs, row(bx), row(lam))


def _pair_block_diag(w):
    nb, bd, _ = w.shape
    wp = w.reshape(nb // 2, 2, bd, bd)
    z = jnp.zeros((nb // 2, bd, bd), w.dtype)
    top = jnp.concatenate([wp[:, 0], z], axis=2)
    bot = jnp.concatenate([z, wp[:, 1]], axis=2)
    return jnp.concatenate([top, bot], axis=1)


SB_TQ = 256
SB_TK = 256
SB_LOG_F32_ZERO = -105.0


def _sb_body(q_ref, k_ref, v_ref, u_ref, o_ref):
    qi = pl.program_id(1)
    q = q_ref[...]
    tri = u_ref[...]
    scale = HEAD_DIM_C ** -0.5
    row_id = lax.broadcasted_iota(jnp.int32, (SB_TQ, SB_TK), 0)
    col_id = lax.broadcasted_iota(jnp.int32, (SB_TQ, SB_TK), 1)
    before = col_id < row_id

    def block(j, carry, acc, diag):
        start = pl.multiple_of(j * SB_TK, SB_TK)
        kb = k_ref[pl.ds(start, SB_TK), :]
        vb = v_ref[pl.ds(start, SB_TK), :]
        z = _dot_nt(q, kb) * scale
        sp = jnp.maximum(z, 0.0) + jnp.log1p(jnp.exp(-jnp.abs(z)))
        lk = -sp
        if diag:
            lk = jnp.where(before, lk, 0.0)
        hi = lk.astype(BF16)
        lo = (lk - hi.astype(F32)).astype(BF16)
        suffix = _dot(hi, tri) + _dot(lo, tri)
        w = jnp.exp((z - sp) + (suffix + carry))
        if diag:
            w = jnp.where(before, w, 0.0)
        acc = acc + _dot(w.astype(BF16), vb)
        carry = carry + jnp.sum(lk, axis=1, keepdims=True)
        return carry, acc

    carry0 = jnp.zeros((SB_TQ, 1), F32)
    acc0 = jnp.zeros((SB_TQ, HEAD_DIM_C), F32)
    carry, acc = block(qi, carry0, acc0, True)

    def live(state):
        j, carry, _ = state
        return jnp.logical_and(j >= 0, jnp.max(carry) > SB_LOG_F32_ZERO)

    def body(state):
        j, carry, acc = state
        carry, acc = block(j, carry, acc, False)
        return j - 1, carry, acc

    _, carry, acc = lax.while_loop(live, body, (qi - 1, carry, acc))
    o_ref[...] = acc.astype(o_ref.dtype)


def _stick_breaking(proj1):
    s = proj1.shape[0]
    hd = HEAD_DIM_C
    jj = jnp.arange(SB_TK)
    tri = (jj[:, None] > jj[None, :]).astype(BF16)
    return pl.pallas_call(
        _sb_body,
        grid=(N_HEADS_C, s // SB_TQ),
        in_specs=[pl.BlockSpec((SB_TQ, hd), lambda h, i: (i, h)),
                  pl.BlockSpec((s, hd), lambda h, i: (0, N_HEADS_C + h)),
                  pl.BlockSpec((s, hd), lambda h, i: (0, 2 * N_HEADS_C + h)),
                  pl.BlockSpec((SB_TK, SB_TK), lambda h, i: (0, 0))],
        out_specs=pl.BlockSpec((SB_TQ, hd), lambda h, i: (i, h)),
        out_shape=jax.ShapeDtypeStruct((s, N_HEADS_C * hd), BF16),
        compiler_params=_params(2),
        name="stick_breaking",
    )(proj1, proj1, proj1, tri)


def kernel(x, rel_bias, norm_mix, even_w_in, even_conv_w, even_conv_b, even_gate_a_w,
           even_gate_a_b, even_gate_x_w, even_gate_x_b, even_lru_lambda, even_sinks,
           even_w_out, odd_w_in, odd_w_out, norm_mlp, w_up, w_down, final_norm):
    b, s, d = x.shape
    h = x.reshape(b * s, d)

    w_in0 = even_w_in[0]
    off_k, off_v, off_x, off_g = Q_COLS, Q_COLS + KV_COLS, Q_COLS + 2 * KV_COLS, Q_COLS + 2 * KV_COLS + LRU_WIDTH
    w_in0 = jnp.concatenate([w_in0[:, :off_k], w_in0[:, off_x:off_g], w_in0[:, off_g:],
                             w_in0[:, off_k:off_v], w_in0[:, off_v:off_x]], axis=1).astype(BF16)
    proj0 = _norm_matmul(h, norm_mix[0], w_in0, F32, tm=1024, tn=512)
    y_a = _swa(proj0, rel_bias, even_sinks[0], _swa_bucket_table())
    y_b = _lru(proj0, even_conv_w[0], even_conv_b[0],
               _pair_block_diag(even_gate_a_w[0]).astype(BF16), even_gate_a_b[0],
               _pair_block_diag(even_gate_x_w[0]).astype(BF16), even_gate_x_b[0],
               even_lru_lambda[0])
    w_out0 = even_w_out[0].astype(BF16)
    h = _matmul_res([y_a, y_b], [w_out0[:Q_COLS], w_out0[Q_COLS:]], h, tm=1024, tn=512)
    h = _mlp(h, norm_mlp[0], w_up[0].astype(BF16), w_down[0].astype(BF16), None, tm=512, tf=512)

    proj1 = _norm_matmul(h, norm_mix[1], odd_w_in[0].astype(BF16), BF16, tm=1024, tn=512)
    y = _stick_breaking(proj1)
    h = _matmul_res([y], [odd_w_out[0].astype(BF16)], h, tm=1024, tn=512)
    h = _mlp(h, norm_mlp[1], w_up[1].astype(BF16), w_down[1].astype(BF16), final_norm,
             tm=512, tf=512)
    return h.reshape(b, s, d)
```

```python
import functools
import math

import jax
import jax.numpy as jnp
from jax import lax
from jax.experimental import pallas as pl
from jax.experimental.pallas import tpu as pltpu

F32 = jnp.float32
BF16 = jnp.bfloat16

D_MODEL = 2048
CHUNK = 64
EPS = 1e-6

HEAD_DIM_A = 64
N_HEADS_A = 16
N_KV_A = 4
GROUP_A = N_HEADS_A // N_KV_A
WINDOW_CHUNKS = 2
Q_BLOCK = 128
HIST = WINDOW_CHUNKS * CHUNK
KEY_SPAN = Q_BLOCK + HIST
NUM_BUCKETS = 32
MAX_DISTANCE = 128

LRU_WIDTH = 1024
LRU_BLOCKS = 16
LRU_BLOCK_DIM = LRU_WIDTH // LRU_BLOCKS
CONV_WIDTH = 4
LRU_C = 8.0

HEAD_DIM_C = 128
N_HEADS_C = 16

Q_COLS = N_HEADS_A * HEAD_DIM_A
KV_COLS = N_KV_A * HEAD_DIM_A
OFF_K = Q_COLS
OFF_V = OFF_K + KV_COLS
OFF_X = OFF_V + KV_COLS
OFF_G = OFF_X + LRU_WIDTH

V7X_LANES = 128
V7X_SUBLANES = 8
V7X_VMEM_LIMIT_BYTES = 56 * 1024 * 1024

MASK_NEG = -1e30

PROJ_TM, PROJ_TN = 1024, 512
MLP_TM, MLP_TF = 512, 1024
LRU_TIME_BLOCK = 256
LRU_CHANNEL_BLOCK = 512
SB_TQ = 256
SB_TK = 256
SB_HEADS_PER_STEP = 2


def _params(n_grid_dims):
    return pltpu.CompilerParams(
        dimension_semantics=("arbitrary",) * n_grid_dims,
        vmem_limit_bytes=V7X_VMEM_LIMIT_BYTES)


def _rms_norm_f32(x, g):
    ms = jnp.mean(x * x, axis=-1, keepdims=True)
    return (x * lax.rsqrt(ms + EPS)) * g


def _dot(a, b):
    return jnp.dot(a, b, preferred_element_type=F32)


def _dot_nt(a, b):
    return lax.dot_general(a, b, (((1,), (1,)), ((), ())), preferred_element_type=F32)


def _norm_matmul_body(x_ref, g_ref, w_ref, o_ref, hn_ref):
    @pl.when(pl.program_id(1) == 0)
    def _():
        hn_ref[...] = _rms_norm_f32(x_ref[...], g_ref[...]).astype(BF16)

    o_ref[...] = _dot(hn_ref[...], w_ref[...].astype(BF16)).astype(o_ref.dtype)


def _norm_matmul(x, g, w, out_dtype):
    m, k = x.shape
    n = w.shape[1]
    tm, tn = PROJ_TM, PROJ_TN
    return pl.pallas_call(
        _norm_matmul_body,
        grid=(m // tm, n // tn),
        in_specs=[pl.BlockSpec((tm, k), lambda i, j: (i, 0)),
                  pl.BlockSpec((1, k), lambda i, j: (0, 0)),
                  pl.BlockSpec((k, tn), lambda i, j: (0, j))],
        out_specs=pl.BlockSpec((tm, tn), lambda i, j: (i, j)),
        out_shape=jax.ShapeDtypeStruct((m, n), out_dtype),
        scratch_shapes=[pltpu.VMEM((tm, k), BF16)],
        compiler_params=_params(2),
        name="norm_matmul",
    )(x, g.reshape(1, k), w)


def _matmul_res_body(*refs, n_parts):
    a_refs = refs[:n_parts]
    w_refs = refs[n_parts:2 * n_parts]
    res_ref, o_ref = refs[2 * n_parts], refs[2 * n_parts + 1]
    acc = res_ref[...]
    for a_ref, w_ref in zip(a_refs, w_refs):
        acc = acc + _dot(a_ref[...], w_ref[...].astype(BF16))
    o_ref[...] = acc


def _matmul_res(a_list, w, res):
    m, n = res.shape
    tm, tn = PROJ_TM, PROJ_TN
    n_parts = len(a_list)
    kp = w.shape[0] // n_parts
    in_specs = [pl.BlockSpec((tm, kp), lambda i, j: (i, 0)) for _ in a_list]
    in_specs += [pl.BlockSpec((kp, tn), functools.partial(lambda i, j, p: (p, j), p=p))
                 for p in range(n_parts)]
    in_specs += [pl.BlockSpec((tm, tn), lambda i, j: (i, j))]
    return pl.pallas_call(
        functools.partial(_matmul_res_body, n_parts=n_parts),
        grid=(m // tm, n // tn),
        in_specs=in_specs,
        out_specs=pl.BlockSpec((tm, tn), lambda i, j: (i, j)),
        out_shape=jax.ShapeDtypeStruct((m, n), F32),
        compiler_params=_params(2),
        name="matmul_res",
    )(*a_list, *([w] * n_parts), res)


def _mlp_body(*refs, final_norm):
    if final_norm:
        h_ref, g_ref, wu_ref, wd_ref, fg_ref, o_ref, hn_ref = refs
    else:
        h_ref, g_ref, wu_ref, wd_ref, o_ref, hn_ref = refs
    f = pl.program_id(1)

    @pl.when(f == 0)
    def _():
        h = h_ref[...]
        hn_ref[...] = _rms_norm_f32(h, g_ref[...]).astype(BF16)
        o_ref[...] = h

    up = _dot(hn_ref[...], wu_ref[...])
    act = jnp.square(jnp.maximum(up, 0.0)).astype(BF16)
    o_ref[...] += _dot(act, wd_ref[...])

    if final_norm:
        @pl.when(f == pl.num_programs(1) - 1)
        def _():
            o_ref[...] = _rms_norm_f32(o_ref[...], fg_ref[...])


def _mlp(h, g, w_up, w_down, final_g):
    m, d = h.shape
    ff = w_up.shape[1]
    tm, tf = MLP_TM, MLP_TF
    final_norm = final_g is not None
    in_specs = [pl.BlockSpec((tm, d), lambda i, f: (i, 0)),
                pl.BlockSpec((1, d), lambda i, f: (0, 0)),
                pl.BlockSpec((d, tf), lambda i, f: (0, f)),
                pl.BlockSpec((tf, d), lambda i, f: (f, 0))]
    args = [h, g.reshape(1, d), w_up, w_down]
    if final_norm:
        in_specs.append(pl.BlockSpec((1, d), lambda i, f: (0, 0)))
        args.append(final_g.reshape(1, d))
    return pl.pallas_call(
        functools.partial(_mlp_body, final_norm=final_norm),
        grid=(m // tm, ff // tf),
        in_specs=in_specs,
        out_specs=pl.BlockSpec((tm, d), lambda i, f: (i, 0)),
        out_shape=jax.ShapeDtypeStruct((m, d), F32),
        scratch_shapes=[pltpu.VMEM((tm, d), BF16)],
        compiler_params=_params(2),
        name="mlp",
    )(*args)


def _swa_body(relb_ref, sink_ref, bucket_ref, q_ref, kc_ref, kp_ref, vc_ref, vp_ref,
              o_ref, tab_ref):
    i = pl.program_id(0)

    @pl.when(i == 0)
    def _():
        bucket = bucket_ref[...]
        for head in range(N_HEADS_A):
            t = jnp.full((Q_BLOCK, KEY_SPAN), MASK_NEG, F32)
            for b in range(NUM_BUCKETS):
                t = jnp.where(bucket == b, relb_ref[b, head], t)
            g = head % GROUP_A
            tab_ref[head // GROUP_A, g * Q_BLOCK:(g + 1) * Q_BLOCK, :] = t

    hist_pen = jnp.where(i == 0, MASK_NEG, 0.0).astype(F32)
    ones = jnp.ones((Q_BLOCK, HEAD_DIM_A), BF16)

    for h in range(N_KV_A):
        cs = slice(h * HEAD_DIM_A, (h + 1) * HEAD_DIM_A)
        kc = kc_ref[:, cs].astype(BF16)
        kp = kp_ref[:, cs].astype(BF16)
        vc = jnp.concatenate([vc_ref[:, cs].astype(BF16), ones], axis=1)
        vp = jnp.concatenate([vp_ref[:, cs].astype(BF16), ones], axis=1)
        heads = [h * GROUP_A + g for g in range(GROUP_A)]
        qs = jnp.concatenate(
            [q_ref[:, hd * HEAD_DIM_A:(hd + 1) * HEAD_DIM_A] for hd in heads], axis=0)
        qs = (qs * (HEAD_DIM_A ** -0.5)).astype(BF16)
        lp = _dot_nt(qs, kp) + tab_ref[h, :, 0:HIST] + hist_pen
        lc = _dot_nt(qs, kc) + tab_ref[h, :, HIST:KEY_SPAN]
        sink = jnp.concatenate(
            [jnp.full((Q_BLOCK, 1), sink_ref[hd], F32) for hd in heads], axis=0)
        m = jnp.maximum(jnp.max(jnp.maximum(lp, lc), axis=1, keepdims=True), sink)
        pp = jnp.exp(lp - m).astype(BF16)
        pc = jnp.exp(lc - m).astype(BF16)
        acc = _dot(pp, vp) + _dot(pc, vc)
        denom = acc[:, HEAD_DIM_A:HEAD_DIM_A + 1] + jnp.exp(sink - m)
        out = acc[:, 0:HEAD_DIM_A] / denom
        for g, hd in enumerate(heads):
            o_ref[:, hd * HEAD_DIM_A:(hd + 1) * HEAD_DIM_A] = (
                out[g * Q_BLOCK:(g + 1) * Q_BLOCK, :].astype(o_ref.dtype))


def _swa(proj0, rel_bias, sinks, bucket):
    s = proj0.shape[0]
    nb = s // Q_BLOCK
    k_blk, v_blk = OFF_K // KV_COLS, OFF_V // KV_COLS
    kv_spec = lambda blk, prev: pl.BlockSpec(
        (Q_BLOCK, KV_COLS),
        (lambda i: (jnp.maximum(i - 1, 0), blk)) if prev else (lambda i: (i, blk)))
    return pl.pallas_call(
        _swa_body,
        grid=(nb,),
        in_specs=[pl.BlockSpec(memory_space=pltpu.SMEM),
                  pl.BlockSpec(memory_space=pltpu.SMEM),
                  pl.BlockSpec((Q_BLOCK, KEY_SPAN), lambda i: (0, 0)),
                  pl.BlockSpec((Q_BLOCK, Q_COLS), lambda i: (i, 0)),
                  kv_spec(k_blk, False), kv_spec(k_blk, True),
                  kv_spec(v_blk, False), kv_spec(v_blk, True)],
        out_specs=pl.BlockSpec((Q_BLOCK, Q_COLS), lambda i: (i, 0)),
        out_shape=jax.ShapeDtypeStruct((s, Q_COLS), BF16),
        scratch_shapes=[pltpu.VMEM((N_KV_A, GROUP_A * Q_BLOCK, KEY_SPAN), F32)],
        compiler_params=_params(1),
        name="swa_attention",
    )(rel_bias, sinks, bucket, proj0, proj0, proj0, proj0, proj0)


def _t5_bucket(rel):
    half = NUM_BUCKETS // 2
    ret = jnp.where(rel > 0, half, 0)
    n = jnp.abs(rel)
    max_exact = half // 2
    nf = jnp.maximum(n, 1).astype(F32)
    large = max_exact + (jnp.log(nf / max_exact) / math.log(MAX_DISTANCE / max_exact)
                         * (half - max_exact)).astype(jnp.int32)
    large = jnp.minimum(large, half - 1)
    return ret + jnp.where(n < max_exact, n, large)


def _swa_bucket_table():
    qi = jnp.arange(Q_BLOCK)
    sj = jnp.arange(KEY_SPAN)
    rel = (sj[None, :] - HIST) - qi[:, None]
    q_chunk = qi // CHUNK
    k_chunk = (sj - HIST) // CHUNK
    in_window = ((k_chunk[None, :] <= q_chunk[:, None])
                 & (k_chunk[None, :] >= q_chunk[:, None] - WINDOW_CHUNKS))
    return jnp.where(in_window, _t5_bucket(rel), -1).astype(jnp.int32)


LRU_PAD = V7X_SUBLANES
LRU_LANE_TILES = LRU_CHANNEL_BLOCK // V7X_LANES


def _neg_expm1(y):
    e = jnp.exp(y)
    safe_log = jnp.where(e == 1.0, 1.0, jnp.log(e))
    near_zero = jnp.where(e == 1.0, -y, (1.0 - e) * y / safe_log)
    return jnp.where(y > -0.5, near_zero, 1.0 - e)


def _lru_body(xb_ref, gb_ref, cw_ref, cb_ref, wa_ref, ba_ref, wx_ref, bx_ref, lam_ref,
              o_ref, xpad_ref, a_ref, u_ref, hs_ref, hc_ref):
    t = pl.program_id(1)
    tb, cb = LRU_TIME_BLOCK, LRU_CHANNEL_BLOCK

    @pl.when(t == 0)
    def _():
        xpad_ref[0:LRU_PAD, :] = jnp.zeros((LRU_PAD, cb), F32)
        hc_ref[...] = jnp.zeros((1, cb), F32)

    x = xb_ref[...]
    xpad_ref[LRU_PAD:LRU_PAD + tb, :] = x
    xc = cb_ref[...] + cw_ref[CONV_WIDTH - 1:CONV_WIDTH, :] * x
    for w in range(CONV_WIDTH - 1):
        off = LRU_PAD - (CONV_WIDTH - 1) + w
        xc = xc + cw_ref[w:w + 1, :] * xpad_ref[off:off + tb, :]
    xpad_ref[0:LRU_PAD, :] = x[tb - LRU_PAD:tb, :]

    neg_c_softplus = -LRU_C * jax.nn.softplus(-lam_ref[...])
    for c in range(LRU_LANE_TILES):
        cs = slice(c * V7X_LANES, (c + 1) * V7X_LANES)
        xcc = xc[:, cs]
        xcb = xcc.astype(BF16)
        r = jax.nn.sigmoid(_dot(xcb, wa_ref[c]) + ba_ref[:, cs])
        ig = jax.nn.sigmoid(_dot(xcb, wx_ref[c]) + bx_ref[:, cs])
        log_a = r * neg_c_softplus[:, cs]
        a_ref[:, cs] = jnp.exp(log_a)
        u_ref[:, cs] = jnp.sqrt(_neg_expm1(2.0 * log_a)) * (ig * xcc)

    def scan_rows(r8, h):
        base = pl.multiple_of(r8 * V7X_SUBLANES, V7X_SUBLANES)
        for r in range(V7X_SUBLANES):
            h = a_ref[pl.ds(base + r, 1), :] * h + u_ref[pl.ds(base + r, 1), :]
            hs_ref[pl.ds(base + r, 1), :] = h
        return h

    hc_ref[...] = lax.fori_loop(0, tb // V7X_SUBLANES, scan_rows, hc_ref[...])
    o_ref[...] = (jax.nn.gelu(gb_ref[...], approximate=True) * hs_ref[...]).astype(o_ref.dtype)


def _lru(proj0, conv_w, conv_b, wa_pairs, ba, wx_pairs, bx, lam):
    s = proj0.shape[0]
    tb, cb = LRU_TIME_BLOCK, LRU_CHANNEL_BLOCK
    row = lambda v: v.reshape(1, LRU_WIDTH)
    vec_spec = pl.BlockSpec((1, cb), lambda c, t: (0, c))
    gate_spec = pl.BlockSpec((LRU_LANE_TILES, V7X_LANES, V7X_LANES), lambda c, t: (c, 0, 0))
    x_blk, g_blk = OFF_X // cb, OFF_G // cb
    return pl.pallas_call(
        _lru_body,
        grid=(LRU_WIDTH // cb, s // tb),
        in_specs=[pl.BlockSpec((tb, cb), lambda c, t: (t, x_blk + c)),
                  pl.BlockSpec((tb, cb), lambda c, t: (t, g_blk + c)),
                  pl.BlockSpec((CONV_WIDTH, cb), lambda c, t: (0, c)), vec_spec,
                  gate_spec, vec_spec, gate_spec, vec_spec, vec_spec],
        out_specs=pl.BlockSpec((tb, cb), lambda c, t: (t, c)),
        out_shape=jax.ShapeDtypeStruct((s, LRU_WIDTH), BF16),
        scratch_shapes=[pltpu.VMEM((tb + LRU_PAD, cb), F32),
                        pltpu.VMEM((tb, cb), F32), pltpu.VMEM((tb, cb), F32),
                        pltpu.VMEM((tb, cb), F32), pltpu.VMEM((1, cb), F32)],
        compiler_params=_params(2),
        name="rg_lru",
    )(proj0, proj0, conv_w, row(conv_b), wa_pairs, row(ba), wx_pairs, row(bx), row(lam))


def _pair_block_diag(w):
    nb, bd, _ = w.shape
    wp = w.reshape(nb // 2, 2, bd, bd)
    z = jnp.zeros((nb // 2, bd, bd), w.dtype)
    top = jnp.concatenate([wp[:, 0], z], axis=2)
    bot = jnp.concatenate([z, wp[:, 1]], axis=2)
    return jnp.concatenate([top, bot], axis=1)


SB_LOG_F32_ZERO = -105.0


def _sb_scores(q, kb, tri2, mask):
    z = _dot_nt(q, kb) * (HEAD_DIM_C ** -0.5)
    sp = jnp.maximum(z, 0.0) + jnp.log(1.0 + jnp.exp(-jnp.abs(z)))
    if mask is not None:
        sp = jnp.where(mask, sp, 0.0)
    hi = sp.astype(BF16)
    lo = (sp - hi.astype(F32)).astype(BF16)
    return z, _dot(jnp.concatenate([hi, lo], axis=1), tri2)


def _sb_weighted(z, incl, carry, vb, mask):
    w = jnp.exp(z - (incl + carry))
    if mask is not None:
        w = jnp.where(mask, w, 0.0)
    return _dot(w.astype(BF16), vb)


def _sb_body(q_ref, k_ref, v_ref, tri_ref, o_ref):
    qi = pl.program_id(1)
    hd = HEAD_DIM_C
    heads = range(SB_HEADS_PER_STEP)
    qs = [q_ref[:, h * hd:(h + 1) * hd] for h in heads]
    tri2 = tri_ref[...]
    row_id = lax.broadcasted_iota(jnp.int32, (SB_TQ, SB_TK), 0)
    col_id = lax.broadcasted_iota(jnp.int32, (SB_TQ, SB_TK), 1)
    before = col_id < row_id

    def kv(j):
        start = pl.multiple_of(j * SB_TK, SB_TK)
        return ([k_ref[pl.ds(start, SB_TK), h * hd:(h + 1) * hd] for h in heads],
                [v_ref[pl.ds(start, SB_TK), h * hd:(h + 1) * hd] for h in heads])

    has_prev = qi > 0
    kd, vd = kv(qi)
    kp, vp = kv(jnp.maximum(qi - 1, 0))
    sd = [_sb_scores(qs[h], kd[h], tri2, before) for h in heads]
    sprev = [_sb_scores(qs[h], kp[h], tri2, None) for h in heads]
    state = ()
    for h in heads:
        (zd, incl_d), (zp, incl_p) = sd[h], sprev[h]
        carry = incl_d[:, 0:1]
        acc = _sb_weighted(zd, incl_d, 0.0, vd[h], before)
        acc_p = _sb_weighted(zp, incl_p, carry, vp[h], None)
        acc = acc + jnp.where(has_prev, acc_p, 0.0)
        carry = carry + jnp.where(has_prev, incl_p[:, 0:1], 0.0)
        state += (carry, acc)

    def live(loop_state):
        j, state = loop_state
        stick = jnp.minimum(*[jnp.min(state[2 * h]) for h in heads])
        return jnp.logical_and(j >= 0, -stick > SB_LOG_F32_ZERO)

    def body(loop_state):
        j, state = loop_state
        kb, vb = kv(j)
        scores = [_sb_scores(qs[h], kb[h], tri2, None) for h in heads]
        out = ()
        for h in heads:
            z, incl = scores[h]
            carry, acc = state[2 * h], state[2 * h + 1]
            out += (carry + incl[:, 0:1], acc + _sb_weighted(z, incl, carry, vb[h], None))
        return j - 1, out

    _, state = lax.while_loop(live, body, (qi - 2, state))
    for h in heads:
        o_ref[:, h * hd:(h + 1) * hd] = state[2 * h + 1].astype(o_ref.dtype)


def _stick_breaking(proj1):
    s = proj1.shape[0]
    hw = SB_HEADS_PER_STEP * HEAD_DIM_C
    n_groups = N_HEADS_C // SB_HEADS_PER_STEP
    jj = jnp.arange(SB_TK)
    tri = (jj[:, None] >= jj[None, :]).astype(BF16)
    tri2 = jnp.concatenate([tri, tri], axis=0)
    return pl.pallas_call(
        _sb_body,
        grid=(n_groups, s // SB_TQ),
        in_specs=[pl.BlockSpec((SB_TQ, hw), lambda g, i: (i, g)),
                  pl.BlockSpec((s, hw), lambda g, i: (0, n_groups + g)),
                  pl.BlockSpec((s, hw), lambda g, i: (0, 2 * n_groups + g)),
                  pl.BlockSpec((2 * SB_TK, SB_TK), lambda g, i: (0, 0))],
        out_specs=pl.BlockSpec((SB_TQ, hw), lambda g, i: (i, g)),
        out_shape=jax.ShapeDtypeStruct((s, N_HEADS_C * HEAD_DIM_C), BF16),
        compiler_params=_params(2),
        name="stick_breaking",
    )(proj1, proj1, proj1, tri2)


def kernel(x, rel_bias, norm_mix, even_w_in, even_conv_w, even_conv_b, even_gate_a_w,
           even_gate_a_b, even_gate_x_w, even_gate_x_b, even_lru_lambda, even_sinks,
           even_w_out, odd_w_in, odd_w_out, norm_mlp, w_up, w_down, final_norm):
    b, s, d = x.shape
    h = x.reshape(b * s, d)

    proj0 = _norm_matmul(h, norm_mix[0], even_w_in[0], F32)
    y_a = _swa(proj0, rel_bias, even_sinks[0], _swa_bucket_table())
    y_b = _lru(proj0, even_conv_w[0], even_conv_b[0],
               _pair_block_diag(even_gate_a_w[0]).astype(BF16), even_gate_a_b[0],
               _pair_block_diag(even_gate_x_w[0]).astype(BF16), even_gate_x_b[0],
               even_lru_lambda[0])
    h = _matmul_res([y_a, y_b], even_w_out[0], h)
    h = _mlp(h, norm_mlp[0], w_up[0].astype(BF16), w_down[0].astype(BF16), None)

    proj1 = _norm_matmul(h, norm_mix[1], odd_w_in[0], BF16)
    y = _stick_breaking(proj1)
    h = _matmul_res([y], odd_w_out[0], h)
    h = _mlp(h, norm_mlp[1], w_up[1].astype(BF16), w_down[1].astype(BF16), final_norm)
    return h.reshape(b, s, d)
```

```python
import functools
import math

import jax
import jax.numpy as jnp
from jax import lax
from jax.experimental import pallas as pl
from jax.experimental.pallas import tpu as pltpu

F32 = jnp.float32
BF16 = jnp.bfloat16

D_MODEL = 2048
CHUNK = 64
EPS = 1e-6

HEAD_DIM_A = 64
N_HEADS_A = 16
N_KV_A = 4
GROUP_A = N_HEADS_A // N_KV_A
WINDOW_CHUNKS = 2
Q_BLOCK = 128
HIST = WINDOW_CHUNKS * CHUNK
KEY_SPAN = Q_BLOCK + HIST
NUM_BUCKETS = 32
MAX_DISTANCE = 128

LRU_WIDTH = 1024
LRU_BLOCKS = 16
LRU_BLOCK_DIM = LRU_WIDTH // LRU_BLOCKS
CONV_WIDTH = 4
LRU_C = 8.0

HEAD_DIM_C = 128
N_HEADS_C = 16

Q_COLS = N_HEADS_A * HEAD_DIM_A
KV_COLS = N_KV_A * HEAD_DIM_A
OFF_K = Q_COLS
OFF_V = OFF_K + KV_COLS
OFF_X = OFF_V + KV_COLS
OFF_G = OFF_X + LRU_WIDTH

V7X_LANES = 128
V7X_SUBLANES = 8
V7X_VMEM_LIMIT_BYTES = 56 * 1024 * 1024

MASK_NEG = -1e30

PROJ_TM = 1024
PROJ_TN_MAX = 1024
MLP_TM, MLP_TF = 1024, 512
LRU_TIME_BLOCK = 256
LRU_COL_BLOCK = 512
SB_TQ = 256
SB_TK = 256
SB_HEADS_PER_STEP = 2


def _params(n_grid_dims):
    return pltpu.CompilerParams(
        dimension_semantics=("arbitrary",) * n_grid_dims,
        vmem_limit_bytes=V7X_VMEM_LIMIT_BYTES)


def _proj_tn(n):
    tn = PROJ_TN_MAX
    while n % tn:
        tn //= 2
    return tn


def _rms_norm_f32(x, g):
    ms = jnp.mean(x * x, axis=-1, keepdims=True)
    return (x * lax.rsqrt(ms + EPS)) * g


def _dot(a, b):
    return jnp.dot(a, b, preferred_element_type=F32)


def _dot_nt(a, b):
    return lax.dot_general(a, b, (((1,), (1,)), ((), ())), preferred_element_type=F32)


def _norm_matmul_body(x_ref, g_ref, w_ref, o_ref, hn_ref):
    @pl.when(pl.program_id(1) == 0)
    def _():
        hn_ref[...] = _rms_norm_f32(x_ref[...], g_ref[...]).astype(BF16)

    o_ref[...] = _dot(hn_ref[...], w_ref[...].astype(BF16)).astype(o_ref.dtype)


def _norm_matmul(x, g, w, out_dtype):
    m, k = x.shape
    n = w.shape[1]
    tm, tn = PROJ_TM, _proj_tn(n)
    return pl.pallas_call(
        _norm_matmul_body,
        grid=(m // tm, n // tn),
        in_specs=[pl.BlockSpec((tm, k), lambda i, j: (i, 0)),
                  pl.BlockSpec((1, k), lambda i, j: (0, 0)),
                  pl.BlockSpec((k, tn), lambda i, j: (0, j))],
        out_specs=pl.BlockSpec((tm, tn), lambda i, j: (i, j)),
        out_shape=jax.ShapeDtypeStruct((m, n), out_dtype),
        scratch_shapes=[pltpu.VMEM((tm, k), BF16)],
        compiler_params=_params(2),
        name="norm_matmul",
    )(x, g.reshape(1, k), w)


def _matmul_res_body(*refs, n_parts):
    a_refs = refs[:n_parts]
    w_refs = refs[n_parts:2 * n_parts]
    res_ref, o_ref = refs[2 * n_parts], refs[2 * n_parts + 1]
    acc = res_ref[...]
    for a_ref, w_ref in zip(a_refs, w_refs):
        acc = acc + _dot(a_ref[...], w_ref[...].astype(BF16))
    o_ref[...] = acc


def _matmul_res(a_list, w, res):
    m, n = res.shape
    tm, tn = PROJ_TM, _proj_tn(n)
    n_parts = len(a_list)
    kp = w.shape[0] // n_parts
    in_specs = [pl.BlockSpec((tm, kp), lambda i, j: (i, 0)) for _ in a_list]
    in_specs += [pl.BlockSpec((kp, tn), functools.partial(lambda i, j, p: (p, j), p=p))
                 for p in range(n_parts)]
    in_specs += [pl.BlockSpec((tm, tn), lambda i, j: (i, j))]
    return pl.pallas_call(
        functools.partial(_matmul_res_body, n_parts=n_parts),
        grid=(m // tm, n // tn),
        in_specs=in_specs,
        out_specs=pl.BlockSpec((tm, tn), lambda i, j: (i, j)),
        out_shape=jax.ShapeDtypeStruct((m, n), F32),
        compiler_params=_params(2),
        name="matmul_res",
    )(*a_list, *([w] * n_parts), res)


def _mlp_body(*refs, final_norm):
    if final_norm:
        h_ref, g_ref, wu_ref, wd_ref, fg_ref, o_ref, hn_ref = refs
    else:
        h_ref, g_ref, wu_ref, wd_ref, o_ref, hn_ref = refs
    f = pl.program_id(1)

    @pl.when(f == 0)
    def _():
        h = h_ref[...]
        hn_ref[...] = _rms_norm_f32(h, g_ref[...]).astype(BF16)
        o_ref[...] = h

    up = _dot(hn_ref[...], wu_ref[...].astype(BF16))
    act = jnp.square(jnp.maximum(up, 0.0)).astype(BF16)
    o_ref[...] += _dot(act, wd_ref[...].astype(BF16))

    if final_norm:
        @pl.when(f == pl.num_programs(1) - 1)
        def _():
            o_ref[...] = _rms_norm_f32(o_ref[...], fg_ref[...])


def _mlp(h, g, w_up, w_down, layer, final_g):
    m, d = h.shape
    ff = w_up.shape[2]
    tm, tf = MLP_TM, MLP_TF
    final_norm = final_g is not None
    once = pl.Buffered(1)
    in_specs = [pl.BlockSpec((tm, d), lambda i, f: (i, 0), pipeline_mode=once),
                pl.BlockSpec((1, d), lambda i, f: (0, 0)),
                pl.BlockSpec((None, d, tf), lambda i, f: (layer, 0, f)),
                pl.BlockSpec((None, tf, d), lambda i, f: (layer, f, 0))]
    args = [h, g.reshape(1, d), w_up, w_down]
    if final_norm:
        in_specs.append(pl.BlockSpec((1, d), lambda i, f: (0, 0)))
        args.append(final_g.reshape(1, d))
    return pl.pallas_call(
        functools.partial(_mlp_body, final_norm=final_norm),
        grid=(m // tm, ff // tf),
        in_specs=in_specs,
        out_specs=pl.BlockSpec((tm, d), lambda i, f: (i, 0), pipeline_mode=once),
        out_shape=jax.ShapeDtypeStruct((m, d), F32),
        scratch_shapes=[pltpu.VMEM((tm, d), BF16)],
        compiler_params=_params(2),
        name="mlp",
    )(*args)


def _swa_body(relb_ref, sink_ref, bucket_ref, q_ref, kc_ref, kp_ref, vc_ref, vp_ref,
              o_ref, tab_ref):
    i = pl.program_id(0)

    @pl.when(i == 0)
    def _():
        bucket = bucket_ref[...]
        for head in range(N_HEADS_A):
            t = jnp.full((Q_BLOCK, KEY_SPAN), MASK_NEG, F32)
            for b in range(NUM_BUCKETS):
                t = jnp.where(bucket == b, relb_ref[b, head], t)
            g = head % GROUP_A
            tab_ref[head // GROUP_A, g * Q_BLOCK:(g + 1) * Q_BLOCK, :] = t

    hist_pen = jnp.where(i == 0, MASK_NEG, 0.0).astype(F32)
    ones = jnp.ones((Q_BLOCK, HEAD_DIM_A), BF16)

    for h in range(N_KV_A):
        cs = slice(h * HEAD_DIM_A, (h + 1) * HEAD_DIM_A)
        kc = kc_ref[:, cs].astype(BF16)
        kp = kp_ref[:, cs].astype(BF16)
        vc = jnp.concatenate([vc_ref[:, cs].astype(BF16), ones], axis=1)
        vp = jnp.concatenate([vp_ref[:, cs].astype(BF16), ones], axis=1)
        heads = [h * GROUP_A + g for g in range(GROUP_A)]
        qs = jnp.concatenate(
            [q_ref[:, hd * HEAD_DIM_A:(hd + 1) * HEAD_DIM_A] for hd in heads], axis=0)
        qs = (qs * (HEAD_DIM_A ** -0.5)).astype(BF16)
        lp = _dot_nt(qs, kp) + tab_ref[h, :, 0:HIST] + hist_pen
        lc = _dot_nt(qs, kc) + tab_ref[h, :, HIST:KEY_SPAN]
        sink = jnp.concatenate(
            [jnp.full((Q_BLOCK, 1), sink_ref[hd], F32) for hd in heads], axis=0)
        m = jnp.maximum(jnp.max(jnp.maximum(lp, lc), axis=1, keepdims=True), sink)
        pp = jnp.exp(lp - m).astype(BF16)
        pc = jnp.exp(lc - m).astype(BF16)
        acc = _dot(pp, vp) + _dot(pc, vc)
        denom = acc[:, HEAD_DIM_A:HEAD_DIM_A + 1] + jnp.exp(sink - m)
        out = acc[:, 0:HEAD_DIM_A] / denom
        for g, hd in enumerate(heads):
            o_ref[:, hd * HEAD_DIM_A:(hd + 1) * HEAD_DIM_A] = (
                out[g * Q_BLOCK:(g + 1) * Q_BLOCK, :].astype(o_ref.dtype))


def _swa(proj0, rel_bias, sinks, bucket):
    s = proj0.shape[0]
    nb = s // Q_BLOCK
    k_blk, v_blk = OFF_K // KV_COLS, OFF_V // KV_COLS
    kv_spec = lambda blk, prev: pl.BlockSpec(
        (Q_BLOCK, KV_COLS),
        (lambda i: (jnp.maximum(i - 1, 0), blk)) if prev else (lambda i: (i, blk)))
    return pl.pallas_call(
        _swa_body,
        grid=(nb,),
        in_specs=[pl.BlockSpec(memory_space=pltpu.SMEM),
                  pl.BlockSpec(memory_space=pltpu.SMEM),
                  pl.BlockSpec((Q_BLOCK, KEY_SPAN), lambda i: (0, 0)),
                  pl.BlockSpec((Q_BLOCK, Q_COLS), lambda i: (i, 0)),
                  kv_spec(k_blk, False), kv_spec(k_blk, True),
                  kv_spec(v_blk, False), kv_spec(v_blk, True)],
        out_specs=pl.BlockSpec((Q_BLOCK, Q_COLS), lambda i: (i, 0)),
        out_shape=jax.ShapeDtypeStruct((s, Q_COLS), BF16),
        scratch_shapes=[pltpu.VMEM((N_KV_A, GROUP_A * Q_BLOCK, KEY_SPAN), F32)],
        compiler_params=_params(1),
        name="swa_attention",
    )(rel_bias, sinks, bucket, proj0, proj0, proj0, proj0, proj0)


def _t5_bucket(rel):
    half = NUM_BUCKETS // 2
    ret = jnp.where(rel > 0, half, 0)
    n = jnp.abs(rel)
    max_exact = half // 2
    nf = jnp.maximum(n, 1).astype(F32)
    large = max_exact + (jnp.log(nf / max_exact) / math.log(MAX_DISTANCE / max_exact)
                         * (half - max_exact)).astype(jnp.int32)
    large = jnp.minimum(large, half - 1)
    return ret + jnp.where(n < max_exact, n, large)


def _swa_bucket_table():
    qi = jnp.arange(Q_BLOCK)
    sj = jnp.arange(KEY_SPAN)
    rel = (sj[None, :] - HIST) - qi[:, None]
    q_chunk = qi // CHUNK
    k_chunk = (sj - HIST) // CHUNK
    in_window = ((k_chunk[None, :] <= q_chunk[:, None])
                 & (k_chunk[None, :] >= q_chunk[:, None] - WINDOW_CHUNKS))
    return jnp.where(in_window, _t5_bucket(rel), -1).astype(jnp.int32)


LRU_PAD = V7X_SUBLANES
LRU_LANE_TILES = LRU_WIDTH // V7X_LANES
LRU_COL_BLOCKS = LRU_WIDTH // LRU_COL_BLOCK


def _neg_expm1(y):
    e = jnp.exp(y)
    safe_log = jnp.where(e == 1.0, 1.0, jnp.log(e))
    near_zero = jnp.where(e == 1.0, -y, (1.0 - e) * y / safe_log)
    return jnp.where(y > -0.5, near_zero, 1.0 - e)


def _lru_body(*refs):
    xb_refs = refs[:LRU_COL_BLOCKS]
    gb_refs = refs[LRU_COL_BLOCKS:2 * LRU_COL_BLOCKS]
    (cw_ref, cb_ref, wa_ref, ba_ref, wx_ref, bx_ref, lam_ref,
     o_ref, xpad_ref, a_ref, u_ref, hs_ref, hc_ref) = refs[2 * LRU_COL_BLOCKS:]
    t = pl.program_id(0)
    tb, cb = LRU_TIME_BLOCK, LRU_WIDTH

    @pl.when(t == 0)
    def _():
        xpad_ref[0:LRU_PAD, :] = jnp.zeros((LRU_PAD, cb), F32)
        hc_ref[...] = jnp.zeros((1, cb), F32)

    x = jnp.concatenate([r[...] for r in xb_refs], axis=1)
    gb = jnp.concatenate([r[...] for r in gb_refs], axis=1)
    xpad_ref[LRU_PAD:LRU_PAD + tb, :] = x
    xc = cb_ref[...] + cw_ref[CONV_WIDTH - 1:CONV_WIDTH, :] * x
    for w in range(CONV_WIDTH - 1):
        off = LRU_PAD - (CONV_WIDTH - 1) + w
        xc = xc + cw_ref[w:w + 1, :] * xpad_ref[off:off + tb, :]
    xpad_ref[0:LRU_PAD, :] = x[tb - LRU_PAD:tb, :]

    neg_c_softplus = -LRU_C * jax.nn.softplus(-lam_ref[...])
    for c in range(LRU_LANE_TILES):
        cs = slice(c * V7X_LANES, (c + 1) * V7X_LANES)
        xcc = xc[:, cs]
        xcb = xcc.astype(BF16)
        r = jax.nn.sigmoid(_dot(xcb, wa_ref[c]) + ba_ref[:, cs])
        ig = jax.nn.sigmoid(_dot(xcb, wx_ref[c]) + bx_ref[:, cs])
        log_a = r * neg_c_softplus[:, cs]
        a_ref[:, cs] = jnp.exp(log_a)
        u_ref[:, cs] = jnp.sqrt(_neg_expm1(2.0 * log_a)) * (ig * xcc)

    def scan_rows(r8, h):
        base = pl.multiple_of(r8 * V7X_SUBLANES, V7X_SUBLANES)
        for r in range(V7X_SUBLANES):
            h = a_ref[pl.ds(base + r, 1), :] * h + u_ref[pl.ds(base + r, 1), :]
            hs_ref[pl.ds(base + r, 1), :] = h
        return h

    hc_ref[...] = lax.fori_loop(0, tb // V7X_SUBLANES, scan_rows, hc_ref[...])
    o_ref[...] = (jax.nn.gelu(gb, approximate=True) * hs_ref[...]).astype(o_ref.dtype)


def _lru(proj0, conv_w, conv_b, wa_pairs, ba, wx_pairs, bx, lam):
    s = proj0.shape[0]
    tb, cb, wb = LRU_TIME_BLOCK, LRU_WIDTH, LRU_COL_BLOCK
    row = lambda v: v.reshape(1, cb)
    whole = lambda shape: pl.BlockSpec(shape, lambda t: (0,) * len(shape))
    col_blocks = lambda off: [
        pl.BlockSpec((tb, wb), functools.partial(lambda t, c: (t, c), c=off // wb + c))
        for c in range(LRU_COL_BLOCKS)]
    return pl.pallas_call(
        _lru_body,
        grid=(s // tb,),
        in_specs=col_blocks(OFF_X) + col_blocks(OFF_G) + [
            whole((CONV_WIDTH, cb)), whole((1, cb)),
            whole(wa_pairs.shape), whole((1, cb)),
            whole(wx_pairs.shape), whole((1, cb)), whole((1, cb))],
        out_specs=pl.BlockSpec((tb, cb), lambda t: (t, 0)),
        out_shape=jax.ShapeDtypeStruct((s, cb), BF16),
        scratch_shapes=[pltpu.VMEM((tb + LRU_PAD, cb), F32),
                        pltpu.VMEM((tb, cb), F32), pltpu.VMEM((tb, cb), F32),
                        pltpu.VMEM((tb, cb), F32), pltpu.VMEM((1, cb), F32)],
        compiler_params=_params(1),
        name="rg_lru",
    )(*([proj0] * (2 * LRU_COL_BLOCKS)), conv_w, row(conv_b), wa_pairs, row(ba),
      wx_pairs, row(bx), row(lam))


def _pair_block_diag(w):
    nb, bd, _ = w.shape
    wp = w.reshape(nb // 2, 2, bd, bd)
    z = jnp.zeros((nb // 2, bd, bd), w.dtype)
    top = jnp.concatenate([wp[:, 0], z], axis=2)
    bot = jnp.concatenate([z, wp[:, 1]], axis=2)
    return jnp.concatenate([top, bot], axis=1)


SB_LOG_F32_ZERO = -105.0


def _sb_scores(q, kb, tri2, mask):
    z = _dot_nt(q, kb) * (HEAD_DIM_C ** -0.5)
    sp = jnp.maximum(z, 0.0) + jnp.log(1.0 + jnp.exp(-jnp.abs(z)))
    if mask is not None:
        sp = jnp.where(mask, sp, 0.0)
    hi = sp.astype(BF16)
    lo = (sp - hi.astype(F32)).astype(BF16)
    return z, _dot(jnp.concatenate([hi, lo], axis=1), tri2)


def _sb_weighted(z, incl, carry, vb, mask):
    w = jnp.exp(z - (incl + carry))
    if mask is not None:
        w = jnp.where(mask, w, 0.0)
    return _dot(w.astype(BF16), vb)


def _sb_body(q_ref, k_ref, v_ref, tri_ref, o_ref):
    qi = pl.program_id(1)
    hd = HEAD_DIM_C
    heads = range(SB_HEADS_PER_STEP)
    qs = [q_ref[:, h * hd:(h + 1) * hd] for h in heads]
    tri2 = tri_ref[...]
    row_id = lax.broadcasted_iota(jnp.int32, (SB_TQ, SB_TK), 0)
    col_id = lax.broadcasted_iota(jnp.int32, (SB_TQ, SB_TK), 1)
    before = col_id < row_id

    def kv(j):
        start = pl.multiple_of(j * SB_TK, SB_TK)
        return ([k_ref[pl.ds(start, SB_TK), h * hd:(h + 1) * hd] for h in heads],
                [v_ref[pl.ds(start, SB_TK), h * hd:(h + 1) * hd] for h in heads])

    has_prev = qi > 0
    kd, vd = kv(qi)
    kp, vp = kv(jnp.maximum(qi - 1, 0))
    sd = [_sb_scores(qs[h], kd[h], tri2, before) for h in heads]
    sprev = [_sb_scores(qs[h], kp[h], tri2, None) for h in heads]
    state = ()
    for h in heads:
        (zd, incl_d), (zp, incl_p) = sd[h], sprev[h]
        carry = incl_d[:, 0:1]
        acc = _sb_weighted(zd, incl_d, 0.0, vd[h], before)
        acc_p = _sb_weighted(zp, incl_p, carry, vp[h], None)
        acc = acc + jnp.where(has_prev, acc_p, 0.0)
        carry = carry + jnp.where(has_prev, incl_p[:, 0:1], 0.0)
        state += (carry, acc)

    def live(loop_state):
        j, state = loop_state
        stick = jnp.minimum(*[jnp.min(state[2 * h]) for h in heads])
        return jnp.logical_and(j >= 0, -stick > SB_LOG_F32_ZERO)

    def body(loop_state):
        j, state = loop_state
        kb, vb = kv(j)
        scores = [_sb_scores(qs[h], kb[h], tri2, None) for h in heads]
        out = ()
        for h in heads:
            z, incl = scores[h]
            carry, acc = state[2 * h], state[2 * h + 1]
            out += (carry + incl[:, 0:1], acc + _sb_weighted(z, incl, carry, vb[h], None))
        return j - 1, out

    _, state = lax.while_loop(live, body, (qi - 2, state))
    for h in heads:
        o_ref[:, h * hd:(h + 1) * hd] = state[2 * h + 1].astype(o_ref.dtype)


def _stick_breaking(proj1):
    s = proj1.shape[0]
    hw = SB_HEADS_PER_STEP * HEAD_DIM_C
    n_groups = N_HEADS_C // SB_HEADS_PER_STEP
    jj = jnp.arange(SB_TK)
    tri = (jj[:, None] >= jj[None, :]).astype(BF16)
    tri2 = jnp.concatenate([tri, tri], axis=0)
    return pl.pallas_call(
        _sb_body,
        grid=(n_groups, s // SB_TQ),
        in_specs=[pl.BlockSpec((SB_TQ, hw), lambda g, i: (i, g)),
                  pl.BlockSpec((s, hw), lambda g, i: (0, n_groups + g)),
                  pl.BlockSpec((s, hw), lambda g, i: (0, 2 * n_groups + g)),
                  pl.BlockSpec((2 * SB_TK, SB_TK), lambda g, i: (0, 0))],
        out_specs=pl.BlockSpec((SB_TQ, hw), lambda g, i: (i, g)),
        out_shape=jax.ShapeDtypeStruct((s, N_HEADS_C * HEAD_DIM_C), BF16),
        compiler_params=_params(2),
        name="stick_breaking",
    )(proj1, proj1, proj1, tri2)


def kernel(x, rel_bias, norm_mix, even_w_in, even_conv_w, even_conv_b, even_gate_a_w,
           even_gate_a_b, even_gate_x_w, even_gate_x_b, even_lru_lambda, even_sinks,
           even_w_out, odd_w_in, odd_w_out, norm_mlp, w_up, w_down, final_norm):
    b, s, d = x.shape
    h = x.reshape(b * s, d)

    proj0 = _norm_matmul(h, norm_mix[0], even_w_in[0], F32)
    y_a = _swa(proj0, rel_bias, even_sinks[0], _swa_bucket_table())
    y_b = _lru(proj0, even_conv_w[0], even_conv_b[0],
               _pair_block_diag(even_gate_a_w[0]).astype(BF16), even_gate_a_b[0],
               _pair_block_diag(even_gate_x_w[0]).astype(BF16), even_gate_x_b[0],
               even_lru_lambda[0])
    h = _matmul_res([y_a, y_b], even_w_out[0], h)
    h = _mlp(h, norm_mlp[0], w_up, w_down, 0, None)

    proj1 = _norm_matmul(h, norm_mix[1], odd_w_in[0], BF16)
    y = _stick_breaking(proj1)
    h = _matmul_res([y], odd_w_out[0], h)
    h = _mlp(h, norm_mlp[1], w_up, w_down, 1, final_norm)
    return h.reshape(b, s, d)
```

```python
import functools
import math

import jax
import jax.numpy as jnp
from jax import lax
from jax.experimental import pallas as pl
from jax.experimental.pallas import tpu as pltpu

F32 = jnp.float32
BF16 = jnp.bfloat16

D_MODEL = 2048
CHUNK = 64
EPS = 1e-6

HEAD_DIM_A = 64
N_HEADS_A = 16
N_KV_A = 4
GROUP_A = N_HEADS_A // N_KV_A
WINDOW_CHUNKS = 2
Q_BLOCK = 128
HIST = WINDOW_CHUNKS * CHUNK
KEY_SPAN = Q_BLOCK + HIST
NUM_BUCKETS = 32
MAX_DISTANCE = 128

LRU_WIDTH = 1024
LRU_BLOCKS = 16
LRU_BLOCK_DIM = LRU_WIDTH // LRU_BLOCKS
CONV_WIDTH = 4
LRU_C = 8.0

HEAD_DIM_C = 128
N_HEADS_C = 16

Q_COLS = N_HEADS_A * HEAD_DIM_A
KV_COLS = N_KV_A * HEAD_DIM_A
OFF_K = Q_COLS
OFF_V = OFF_K + KV_COLS
OFF_X = OFF_V + KV_COLS
OFF_G = OFF_X + LRU_WIDTH

V7X_LANES = 128
V7X_SUBLANES = 8
V7X_VMEM_LIMIT_BYTES = 56 * 1024 * 1024

MASK_NEG = -1e30

PROJ_TM = 1024
PROJ_TN_MAX = 1024
MLP_TM, MLP_TF = 1024, 512
LRU_TIME_BLOCK = 256
LRU_COL_BLOCK = 512
SB_TQ = 256
SB_TK = 256
SB_HEADS_PER_STEP = 4


def _params(n_grid_dims):
    return pltpu.CompilerParams(
        dimension_semantics=("arbitrary",) * n_grid_dims,
        vmem_limit_bytes=V7X_VMEM_LIMIT_BYTES)


def _proj_tn(n):
    tn = PROJ_TN_MAX
    while n % tn:
        tn //= 2
    return tn


def _rms_norm_f32(x, g):
    ms = jnp.mean(x * x, axis=-1, keepdims=True)
    return (x * lax.rsqrt(ms + EPS)) * g


def _dot(a, b):
    return jnp.dot(a, b, preferred_element_type=F32)


def _dot_nt(a, b):
    return lax.dot_general(a, b, (((1,), (1,)), ((), ())), preferred_element_type=F32)


def _norm_matmul_body(x_ref, g_ref, w_ref, o_ref, hn_ref):
    @pl.when(pl.program_id(1) == 0)
    def _():
        hn_ref[...] = _rms_norm_f32(x_ref[...], g_ref[...]).astype(BF16)

    o_ref[...] = _dot(hn_ref[...], w_ref[...].astype(BF16)).astype(o_ref.dtype)


def _norm_matmul(x, g, w, out_dtype):
    m, k = x.shape
    n = w.shape[1]
    tm, tn = PROJ_TM, _proj_tn(n)
    return pl.pallas_call(
        _norm_matmul_body,
        grid=(m // tm, n // tn),
        in_specs=[pl.BlockSpec((tm, k), lambda i, j: (i, 0)),
                  pl.BlockSpec((1, k), lambda i, j: (0, 0)),
                  pl.BlockSpec((k, tn), lambda i, j: (0, j))],
        out_specs=pl.BlockSpec((tm, tn), lambda i, j: (i, j)),
        out_shape=jax.ShapeDtypeStruct((m, n), out_dtype),
        scratch_shapes=[pltpu.VMEM((tm, k), BF16)],
        compiler_params=_params(2),
        name="norm_matmul",
    )(x, g.reshape(1, k), w)


def _matmul_res_body(*refs, n_parts):
    a_refs = refs[:n_parts]
    w_refs = refs[n_parts:2 * n_parts]
    res_ref, o_ref = refs[2 * n_parts], refs[2 * n_parts + 1]
    acc = res_ref[...]
    for a_ref, w_ref in zip(a_refs, w_refs):
        acc = acc + _dot(a_ref[...], w_ref[...].astype(BF16))
    o_ref[...] = acc


def _matmul_res(a_list, w, res):
    m, n = res.shape
    tm, tn = PROJ_TM, _proj_tn(n)
    n_parts = len(a_list)
    kp = w.shape[0] // n_parts
    in_specs = [pl.BlockSpec((tm, kp), lambda i, j: (i, 0)) for _ in a_list]
    in_specs += [pl.BlockSpec((kp, tn), functools.partial(lambda i, j, p: (p, j), p=p))
                 for p in range(n_parts)]
    in_specs += [pl.BlockSpec((tm, tn), lambda i, j: (i, j))]
    return pl.pallas_call(
        functools.partial(_matmul_res_body, n_parts=n_parts),
        grid=(m // tm, n // tn),
        in_specs=in_specs,
        out_specs=pl.BlockSpec((tm, tn), lambda i, j: (i, j)),
        out_shape=jax.ShapeDtypeStruct((m, n), F32),
        compiler_params=_params(2),
        name="matmul_res",
    )(*a_list, *([w] * n_parts), res)


def _mlp_body(*refs, final_norm):
    if final_norm:
        h_ref, g_ref, wu_ref, wd_ref, fg_ref, o_ref, hn_ref = refs
    else:
        h_ref, g_ref, wu_ref, wd_ref, o_ref, hn_ref = refs
    f = pl.program_id(1)

    @pl.when(f == 0)
    def _():
        h = h_ref[...]
        hn_ref[...] = _rms_norm_f32(h, g_ref[...]).astype(BF16)
        o_ref[...] = h

    up = _dot(hn_ref[...], wu_ref[...].astype(BF16))
    act = jnp.square(jnp.maximum(up, 0.0)).astype(BF16)
    o_ref[...] += _dot(act, wd_ref[...].astype(BF16))

    if final_norm:
        @pl.when(f == pl.num_programs(1) - 1)
        def _():
            o_ref[...] = _rms_norm_f32(o_ref[...], fg_ref[...])


def _mlp(h, g, w_up, w_down, layer, final_g):
    m, d = h.shape
    ff = w_up.shape[2]
    tm, tf = MLP_TM, MLP_TF
    final_norm = final_g is not None
    in_specs = [pl.BlockSpec((tm, d), lambda i, f: (i, 0)),
                pl.BlockSpec((1, d), lambda i, f: (0, 0)),
                pl.BlockSpec((None, d, tf), lambda i, f: (layer, 0, f)),
                pl.BlockSpec((None, tf, d), lambda i, f: (layer, f, 0))]
    args = [h, g.reshape(1, d), w_up, w_down]
    if final_norm:
        in_specs.append(pl.BlockSpec((1, d), lambda i, f: (0, 0)))
        args.append(final_g.reshape(1, d))
    return pl.pallas_call(
        functools.partial(_mlp_body, final_norm=final_norm),
        grid=(m // tm, ff // tf),
        in_specs=in_specs,
        out_specs=pl.BlockSpec((tm, d), lambda i, f: (i, 0)),
        out_shape=jax.ShapeDtypeStruct((m, d), F32),
        scratch_shapes=[pltpu.VMEM((tm, d), BF16)],
        compiler_params=_params(2),
        name="mlp",
    )(*args)


def _swa_body(relb_ref, sink_ref, bucket_ref, q_ref, kc_ref, kp_ref, vc_ref, vp_ref,
              o_ref, tab_ref):
    i = pl.program_id(0)
    hd, nq = HEAD_DIM_A, Q_BLOCK
    lanes = GROUP_A * nq

    @pl.when(i == 0)
    def _():
        bucket = bucket_ref[...]
        for head in range(N_HEADS_A):
            t = jnp.full((KEY_SPAN, nq), MASK_NEG, F32)
            for b in range(NUM_BUCKETS):
                t = jnp.where(bucket == b, relb_ref[b, head], t)
            g = head % GROUP_A
            tab_ref[head // GROUP_A, :, g * nq:(g + 1) * nq] = t

    hist_pen = jnp.where(i == 0, MASK_NEG, 0.0).astype(F32)

    q_t = (q_ref[...] * (hd ** -0.5)).T
    k_all = jnp.concatenate([kp_ref[...], kc_ref[...]], axis=0).astype(BF16)
    v_t = jnp.concatenate([vp_ref[...].T, vc_ref[...].T], axis=1)
    zeros = jnp.zeros((hd, lanes), BF16)
    ones = jnp.ones((2 * V7X_SUBLANES, KEY_SPAN), BF16)

    for h in range(N_KV_A):
        heads = [h * GROUP_A + g for g in range(GROUP_A)]
        qg = jnp.concatenate([q_t[hd_ * hd:(hd_ + 1) * hd, :] for hd_ in heads],
                             axis=1).astype(BF16)
        pair = h // 2
        rhs = jnp.concatenate([qg, zeros] if h % 2 == 0 else [zeros, qg], axis=0)
        logits = _dot(k_all[:, pair * 2 * hd:(pair + 1) * 2 * hd], rhs) + tab_ref[h]
        logits = jnp.concatenate([logits[0:HIST] + hist_pen, logits[HIST:KEY_SPAN]], axis=0)
        sink = jnp.concatenate([jnp.full((1, nq), sink_ref[hd_], F32) for hd_ in heads], axis=1)
        m = jnp.maximum(jnp.max(logits, axis=0, keepdims=True), sink)
        p = jnp.exp(logits - m).astype(BF16)
        v_aug = jnp.concatenate([v_t[h * hd:(h + 1) * hd, :].astype(BF16), ones], axis=0)
        acc = _dot(v_aug, p)
        denom = acc[hd:hd + 1, :] + jnp.exp(sink - m)
        out_t = acc[0:hd, :] * (1.0 / denom)
        for g in range(0, GROUP_A, 2):
            both = jnp.concatenate([out_t[:, g * nq:(g + 1) * nq],
                                    out_t[:, (g + 1) * nq:(g + 2) * nq]], axis=0)
            c0 = heads[g] * hd
            o_ref[:, c0:c0 + 2 * hd] = both.T.astype(o_ref.dtype)


def _swa(proj0, rel_bias, sinks, bucket):
    s = proj0.shape[0]
    nb = s // Q_BLOCK
    k_blk, v_blk = OFF_K // KV_COLS, OFF_V // KV_COLS
    kv_spec = lambda blk, prev: pl.BlockSpec(
        (Q_BLOCK, KV_COLS),
        (lambda i: (jnp.maximum(i - 1, 0), blk)) if prev else (lambda i: (i, blk)))
    return pl.pallas_call(
        _swa_body,
        grid=(nb,),
        in_specs=[pl.BlockSpec(memory_space=pltpu.SMEM),
                  pl.BlockSpec(memory_space=pltpu.SMEM),
                  pl.BlockSpec((KEY_SPAN, Q_BLOCK), lambda i: (0, 0)),
                  pl.BlockSpec((Q_BLOCK, Q_COLS), lambda i: (i, 0)),
                  kv_spec(k_blk, False), kv_spec(k_blk, True),
                  kv_spec(v_blk, False), kv_spec(v_blk, True)],
        out_specs=pl.BlockSpec((Q_BLOCK, Q_COLS), lambda i: (i, 0)),
        out_shape=jax.ShapeDtypeStruct((s, Q_COLS), BF16),
        scratch_shapes=[pltpu.VMEM((N_KV_A, KEY_SPAN, GROUP_A * Q_BLOCK), F32)],
        compiler_params=_params(1),
        name="swa_attention",
    )(rel_bias, sinks, bucket, proj0, proj0, proj0, proj0, proj0)


def _t5_bucket(rel):
    half = NUM_BUCKETS // 2
    ret = jnp.where(rel > 0, half, 0)
    n = jnp.abs(rel)
    max_exact = half // 2
    nf = jnp.maximum(n, 1).astype(F32)
    large = max_exact + (jnp.log(nf / max_exact) / math.log(MAX_DISTANCE / max_exact)
                         * (half - max_exact)).astype(jnp.int32)
    large = jnp.minimum(large, half - 1)
    return ret + jnp.where(n < max_exact, n, large)


def _swa_bucket_table():
    qi = jnp.arange(Q_BLOCK)
    sj = jnp.arange(KEY_SPAN)
    rel = (sj[:, None] - HIST) - qi[None, :]
    q_chunk = qi // CHUNK
    k_chunk = (sj - HIST) // CHUNK
    in_window = ((k_chunk[:, None] <= q_chunk[None, :])
                 & (k_chunk[:, None] >= q_chunk[None, :] - WINDOW_CHUNKS))
    return jnp.where(in_window, _t5_bucket(rel), -1).astype(jnp.int32)


LRU_PAD = V7X_SUBLANES
LRU_LANE_TILES = LRU_WIDTH // V7X_LANES
LRU_COL_BLOCKS = LRU_WIDTH // LRU_COL_BLOCK


def _neg_expm1(y):
    e = jnp.exp(y)
    safe_log = jnp.where(e == 1.0, 1.0, jnp.log(e))
    near_zero = jnp.where(e == 1.0, -y, (1.0 - e) * y / safe_log)
    return jnp.where(y > -0.5, near_zero, 1.0 - e)


def _lru_body(*refs):
    xb_refs = refs[:LRU_COL_BLOCKS]
    gb_refs = refs[LRU_COL_BLOCKS:2 * LRU_COL_BLOCKS]
    (cw_ref, cb_ref, wa_ref, ba_ref, wx_ref, bx_ref, lam_ref,
     o_ref, xpad_ref, a_ref, u_ref, hs_ref, hc_ref) = refs[2 * LRU_COL_BLOCKS:]
    t = pl.program_id(0)
    tb, cb = LRU_TIME_BLOCK, LRU_WIDTH

    @pl.when(t == 0)
    def _():
        xpad_ref[0:LRU_PAD, :] = jnp.zeros((LRU_PAD, cb), F32)
        hc_ref[...] = jnp.zeros((1, cb), F32)

    x = jnp.concatenate([r[...] for r in xb_refs], axis=1)
    gb = jnp.concatenate([r[...] for r in gb_refs], axis=1)
    xpad_ref[LRU_PAD:LRU_PAD + tb, :] = x
    xc = cb_ref[...] + cw_ref[CONV_WIDTH - 1:CONV_WIDTH, :] * x
    for w in range(CONV_WIDTH - 1):
        off = LRU_PAD - (CONV_WIDTH - 1) + w
        xc = xc + cw_ref[w:w + 1, :] * xpad_ref[off:off + tb, :]
    xpad_ref[0:LRU_PAD, :] = x[tb - LRU_PAD:tb, :]

    neg_c_softplus = -LRU_C * jax.nn.softplus(-lam_ref[...])
    for c in range(LRU_LANE_TILES):
        cs = slice(c * V7X_LANES, (c + 1) * V7X_LANES)
        xcc = xc[:, cs]
        xcb = xcc.astype(BF16)
        r = jax.nn.sigmoid(_dot(xcb, wa_ref[c]) + ba_ref[:, cs])
        ig = jax.nn.sigmoid(_dot(xcb, wx_ref[c]) + bx_ref[:, cs])
        log_a = r * neg_c_softplus[:, cs]
        a_ref[:, cs] = jnp.exp(log_a)
        u_ref[:, cs] = jnp.sqrt(_neg_expm1(2.0 * log_a)) * (ig * xcc)

    def scan_rows(r8, h):
        base = pl.multiple_of(r8 * V7X_SUBLANES, V7X_SUBLANES)
        for r in range(V7X_SUBLANES):
            h = a_ref[pl.ds(base + r, 1), :] * h + u_ref[pl.ds(base + r, 1), :]
            hs_ref[pl.ds(base + r, 1), :] = h
        return h

    hc_ref[...] = lax.fori_loop(0, tb // V7X_SUBLANES, scan_rows, hc_ref[...])
    o_ref[...] = (jax.nn.gelu(gb, approximate=True) * hs_ref[...]).astype(o_ref.dtype)


def _lru(proj0, conv_w, conv_b, wa_pairs, ba, wx_pairs, bx, lam):
    s = proj0.shape[0]
    tb, cb, wb = LRU_TIME_BLOCK, LRU_WIDTH, LRU_COL_BLOCK
    row = lambda v: v.reshape(1, cb)
    whole = lambda shape: pl.BlockSpec(shape, lambda t: (0,) * len(shape))
    col_blocks = lambda off: [
        pl.BlockSpec((tb, wb), functools.partial(lambda t, c: (t, c), c=off // wb + c))
        for c in range(LRU_COL_BLOCKS)]
    return pl.pallas_call(
        _lru_body,
        grid=(s // tb,),
        in_specs=col_blocks(OFF_X) + col_blocks(OFF_G) + [
            whole((CONV_WIDTH, cb)), whole((1, cb)),
            whole(wa_pairs.shape), whole((1, cb)),
            whole(wx_pairs.shape), whole((1, cb)), whole((1, cb))],
        out_specs=pl.BlockSpec((tb, cb), lambda t: (t, 0)),
        out_shape=jax.ShapeDtypeStruct((s, cb), BF16),
        scratch_shapes=[pltpu.VMEM((tb + LRU_PAD, cb), F32),
                        pltpu.VMEM((tb, cb), F32), pltpu.VMEM((tb, cb), F32),
                        pltpu.VMEM((tb, cb), F32), pltpu.VMEM((1, cb), F32)],
        compiler_params=_params(1),
        name="rg_lru",
    )(*([proj0] * (2 * LRU_COL_BLOCKS)), conv_w, row(conv_b), wa_pairs, row(ba),
      wx_pairs, row(bx), row(lam))


def _pair_block_diag(w):
    nb, bd, _ = w.shape
    wp = w.reshape(nb // 2, 2, bd, bd)
    z = jnp.zeros((nb // 2, bd, bd), w.dtype)
    top = jnp.concatenate([wp[:, 0], z], axis=2)
    bot = jnp.concatenate([z, wp[:, 1]], axis=2)
    return jnp.concatenate([top, bot], axis=1)


SB_LOG_F32_ZERO = -105.0


def _sb_scores(q, kb, tri2, mask):
    z = _dot_nt(q, kb) * (HEAD_DIM_C ** -0.5)
    sp = jnp.maximum(z, 0.0) + jnp.log(1.0 + jnp.exp(-jnp.abs(z)))
    if mask is not None:
        sp = jnp.where(mask, sp, 0.0)
    hi = sp.astype(BF16)
    lo = (sp - hi.astype(F32)).astype(BF16)
    return z, _dot(jnp.concatenate([hi, lo], axis=1), tri2)


def _sb_weighted(z, incl, carry, vb, mask):
    w = jnp.exp(z - (incl + carry))
    if mask is not None:
        w = jnp.where(mask, w, 0.0)
    return _dot(w.astype(BF16), vb)


def _sb_body(q_ref, k_ref, v_ref, tri_ref, o_ref):
    qi = pl.program_id(1)
    hd = HEAD_DIM_C
    heads = range(SB_HEADS_PER_STEP)
    qs = [q_ref[:, h * hd:(h + 1) * hd] for h in heads]
    tri2 = tri_ref[...]
    row_id = lax.broadcasted_iota(jnp.int32, (SB_TQ, SB_TK), 0)
    col_id = lax.broadcasted_iota(jnp.int32, (SB_TQ, SB_TK), 1)
    before = col_id < row_id

    def kv(j):
        start = pl.multiple_of(j * SB_TK, SB_TK)
        return ([k_ref[pl.ds(start, SB_TK), h * hd:(h + 1) * hd] for h in heads],
                [v_ref[pl.ds(start, SB_TK), h * hd:(h + 1) * hd] for h in heads])

    has_prev = qi > 0
    kd, vd = kv(qi)
    kp, vp = kv(jnp.maximum(qi - 1, 0))
    sd = [_sb_scores(qs[h], kd[h], tri2, before) for h in heads]
    sprev = [_sb_scores(qs[h], kp[h], tri2, None) for h in heads]
    state = ()
    for h in heads:
        (zd, incl_d), (zp, incl_p) = sd[h], sprev[h]
        carry = incl_d[:, 0:1]
        acc = _sb_weighted(zd, incl_d, 0.0, vd[h], before)
        acc_p = _sb_weighted(zp, incl_p, carry, vp[h], None)
        acc = acc + jnp.where(has_prev, acc_p, 0.0)
        carry = carry + jnp.where(has_prev, incl_p[:, 0:1], 0.0)
        state += (carry, acc)

    def live(loop_state):
        j, state = loop_state
        stick = functools.reduce(jnp.minimum, [jnp.min(state[2 * h]) for h in heads])
        return jnp.logical_and(j >= 0, -stick > SB_LOG_F32_ZERO)

    def body(loop_state):
        j, state = loop_state
        kb, vb = kv(j)
        scores = [_sb_scores(qs[h], kb[h], tri2, None) for h in heads]
        out = ()
        for h in heads:
            z, incl = scores[h]
            carry, acc = state[2 * h], state[2 * h + 1]
            out += (carry + incl[:, 0:1], acc + _sb_weighted(z, incl, carry, vb[h], None))
        return j - 1, out

    _, state = lax.while_loop(live, body, (qi - 2, state))
    for h in heads:
        o_ref[:, h * hd:(h + 1) * hd] = state[2 * h + 1].astype(o_ref.dtype)


def _stick_breaking(proj1):
    s = proj1.shape[0]
    hw = SB_HEADS_PER_STEP * HEAD_DIM_C
    n_groups = N_HEADS_C // SB_HEADS_PER_STEP
    jj = jnp.arange(SB_TK)
    tri = (jj[:, None] >= jj[None, :]).astype(BF16)
    tri2 = jnp.concatenate([tri, tri], axis=0)
    return pl.pallas_call(
        _sb_body,
        grid=(n_groups, s // SB_TQ),
        in_specs=[pl.BlockSpec((SB_TQ, hw), lambda g, i: (i, g)),
                  pl.BlockSpec((s, hw), lambda g, i: (0, n_groups + g)),
                  pl.BlockSpec((s, hw), lambda g, i: (0, 2 * n_groups + g)),
                  pl.BlockSpec((2 * SB_TK, SB_TK), lambda g, i: (0, 0))],
        out_specs=pl.BlockSpec((SB_TQ, hw), lambda g, i: (i, g)),
        out_shape=jax.ShapeDtypeStruct((s, N_HEADS_C * HEAD_DIM_C), BF16),
        compiler_params=_params(2),
        name="stick_breaking",
    )(proj1, proj1, proj1, tri2)


def kernel(x, rel_bias, norm_mix, even_w_in, even_conv_w, even_conv_b, even_gate_a_w,
           even_gate_a_b, even_gate_x_w, even_gate_x_b, even_lru_lambda, even_sinks,
           even_w_out, odd_w_in, odd_w_out, norm_mlp, w_up, w_down, final_norm):
    b, s, d = x.shape
    h = x.reshape(b * s, d)

    proj0 = _norm_matmul(h, norm_mix[0], even_w_in[0], F32)
    y_a = _swa(proj0, rel_bias, even_sinks[0], _swa_bucket_table())
    y_b = _lru(proj0, even_conv_w[0], even_conv_b[0],
               _pair_block_diag(even_gate_a_w[0]).astype(BF16), even_gate_a_b[0],
               _pair_block_diag(even_gate_x_w[0]).astype(BF16), even_gate_x_b[0],
               even_lru_lambda[0])
    h = _matmul_res([y_a, y_b], even_w_out[0], h)
    h = _mlp(h, norm_mlp[0], w_up, w_down, 0, None)

    proj1 = _norm_matmul(h, norm_mix[1], odd_w_in[0], BF16)
    y = _stick_breaking(proj1)
    h = _matmul_res([y], odd_w_out[0], h)
    h = _mlp(h, norm_mlp[1], w_up, w_down, 1, final_norm)
    return h.reshape(b, s, d)
```

```python
import functools
import math

import jax
import jax.numpy as jnp
from jax import lax
from jax.experimental import pallas as pl
from jax.experimental.pallas import tpu as pltpu

F32 = jnp.float32
BF16 = jnp.bfloat16

D_MODEL = 2048
CHUNK = 64
EPS = 1e-6

HEAD_DIM_A = 64
N_HEADS_A = 16
N_KV_A = 4
GROUP_A = N_HEADS_A // N_KV_A
WINDOW_CHUNKS = 2
Q_BLOCK = 128
HIST = WINDOW_CHUNKS * CHUNK
KEY_SPAN = Q_BLOCK + HIST
NUM_BUCKETS = 32
MAX_DISTANCE = 128

LRU_WIDTH = 1024
LRU_BLOCKS = 16
LRU_BLOCK_DIM = LRU_WIDTH // LRU_BLOCKS
CONV_WIDTH = 4
LRU_C = 8.0

HEAD_DIM_C = 128
N_HEADS_C = 16

Q_COLS = N_HEADS_A * HEAD_DIM_A
KV_COLS = N_KV_A * HEAD_DIM_A
OFF_K = Q_COLS
OFF_V = OFF_K + KV_COLS
OFF_X = OFF_V + KV_COLS
OFF_G = OFF_X + LRU_WIDTH

V7X_LANES = 128
V7X_SUBLANES = 8
V7X_VMEM_LIMIT_BYTES = 56 * 1024 * 1024

MASK_NEG = -1e30

PROJ_TM = 512
PROJ_TN_MAX = 1024
MLP_TM, MLP_TF = 1024, 512
LRU_TIME_BLOCK = 256
LRU_COL_BLOCK = 512
SB_TQ = 256
SB_TK = 256
SB_HEADS_PER_STEP = 4


def _params(n_grid_dims):
    return pltpu.CompilerParams(
        dimension_semantics=("arbitrary",) * n_grid_dims,
        vmem_limit_bytes=V7X_VMEM_LIMIT_BYTES)


def _proj_tn(n):
    tn = PROJ_TN_MAX
    while n % tn:
        tn //= 2
    return tn


def _rms_norm_f32(x, g):
    ms = jnp.mean(x * x, axis=-1, keepdims=True)
    return (x * lax.rsqrt(ms + EPS)) * g


def _dot(a, b):
    return jnp.dot(a, b, preferred_element_type=F32)


def _dot_nt(a, b):
    return lax.dot_general(a, b, (((1,), (1,)), ((), ())), preferred_element_type=F32)


def _resident(shape):
    return pl.BlockSpec(shape, lambda i: (0,) * len(shape), pipeline_mode=pl.Buffered(1))


def _norm_matmul_body(x_ref, g_ref, w_ref, o_ref, hn_ref):
    hn_ref[...] = _rms_norm_f32(x_ref[...], g_ref[...]).astype(BF16)
    n = o_ref.shape[1]
    tn = _proj_tn(n)
    for c in range(n // tn):
        cs = slice(c * tn, (c + 1) * tn)
        o_ref[:, cs] = _dot(hn_ref[...], w_ref[:, cs]).astype(o_ref.dtype)


def _norm_matmul(x, g, w, out_dtype):
    m, k = x.shape
    n = w.shape[1]
    tm = PROJ_TM
    return pl.pallas_call(
        _norm_matmul_body,
        grid=(m // tm,),
        in_specs=[pl.BlockSpec((tm, k), lambda i: (i, 0)),
                  _resident((1, k)), _resident((k, n))],
        out_specs=pl.BlockSpec((tm, n), lambda i: (i, 0)),
        out_shape=jax.ShapeDtypeStruct((m, n), out_dtype),
        scratch_shapes=[pltpu.VMEM((tm, k), BF16)],
        compiler_params=_params(1),
        name="norm_matmul",
    )(x, g.reshape(1, k), w)


def _matmul_res_body(*refs, n_parts):
    a_refs = refs[:n_parts]
    w_ref, res_ref, o_ref = refs[n_parts:]
    kp = w_ref.shape[0] // n_parts
    n = o_ref.shape[1]
    tn = _proj_tn(n)
    for c in range(n // tn):
        cs = slice(c * tn, (c + 1) * tn)
        acc = res_ref[:, cs]
        for p, a_ref in enumerate(a_refs):
            acc = acc + _dot(a_ref[...], w_ref[p * kp:(p + 1) * kp, cs])
        o_ref[:, cs] = acc


def _matmul_res(a_list, w, res):
    m, n = res.shape
    tm = PROJ_TM
    n_parts = len(a_list)
    kp = w.shape[0] // n_parts
    in_specs = [pl.BlockSpec((tm, kp), lambda i: (i, 0)) for _ in a_list]
    in_specs += [_resident(w.shape), pl.BlockSpec((tm, n), lambda i: (i, 0))]
    return pl.pallas_call(
        functools.partial(_matmul_res_body, n_parts=n_parts),
        grid=(m // tm,),
        in_specs=in_specs,
        out_specs=pl.BlockSpec((tm, n), lambda i: (i, 0)),
        out_shape=jax.ShapeDtypeStruct((m, n), F32),
        compiler_params=_params(1),
        name="matmul_res",
    )(*a_list, w, res)


def _mlp_body(*refs, final_norm):
    if final_norm:
        h_ref, g_ref, wu_ref, wd_ref, fg_ref, o_ref, hn_ref = refs
    else:
        h_ref, g_ref, wu_ref, wd_ref, o_ref, hn_ref = refs
    f = pl.program_id(1)

    @pl.when(f == 0)
    def _():
        h = h_ref[...]
        hn_ref[...] = _rms_norm_f32(h, g_ref[...]).astype(BF16)
        o_ref[...] = h

    up = _dot(hn_ref[...], wu_ref[...].astype(BF16))
    act = jnp.square(jnp.maximum(up, 0.0)).astype(BF16)
    o_ref[...] += _dot(act, wd_ref[...].astype(BF16))

    if final_norm:
        @pl.when(f == pl.num_programs(1) - 1)
        def _():
            o_ref[...] = _rms_norm_f32(o_ref[...], fg_ref[...])


def _mlp(h, g, w_up, w_down, layer, final_g):
    m, d = h.shape
    ff = w_up.shape[2]
    tm, tf = MLP_TM, MLP_TF
    final_norm = final_g is not None
    in_specs = [pl.BlockSpec((tm, d), lambda i, f: (i, 0)),
                pl.BlockSpec((1, d), lambda i, f: (0, 0)),
                pl.BlockSpec((None, d, tf), lambda i, f: (layer, 0, f)),
                pl.BlockSpec((None, tf, d), lambda i, f: (layer, f, 0))]
    args = [h, g.reshape(1, d), w_up, w_down]
    if final_norm:
        in_specs.append(pl.BlockSpec((1, d), lambda i, f: (0, 0)))
        args.append(final_g.reshape(1, d))
    return pl.pallas_call(
        functools.partial(_mlp_body, final_norm=final_norm),
        grid=(m // tm, ff // tf),
        in_specs=in_specs,
        out_specs=pl.BlockSpec((tm, d), lambda i, f: (i, 0)),
        out_shape=jax.ShapeDtypeStruct((m, d), F32),
        scratch_shapes=[pltpu.VMEM((tm, d), BF16)],
        compiler_params=_params(2),
        name="mlp",
    )(*args)


def _swa_body(relb_ref, sink_ref, bucket_ref, q_ref, kc_ref, kp_ref, vc_ref, vp_ref,
              o_ref, tab_ref):
    i = pl.program_id(0)
    hd, nq = HEAD_DIM_A, Q_BLOCK
    lanes = GROUP_A * nq

    @pl.when(i == 0)
    def _():
        bucket = bucket_ref[...]
        for head in range(N_HEADS_A):
            t = jnp.full((KEY_SPAN, nq), MASK_NEG, F32)
            for b in range(NUM_BUCKETS):
                t = jnp.where(bucket == b, relb_ref[b, head], t)
            g = head % GROUP_A
            tab_ref[head // GROUP_A, :, g * nq:(g + 1) * nq] = t

    hist_pen = jnp.where(i == 0, MASK_NEG, 0.0).astype(F32)

    q_t = (q_ref[...] * (hd ** -0.5)).T
    k_all = jnp.concatenate([kp_ref[...], kc_ref[...]], axis=0).astype(BF16)
    v_t = jnp.concatenate([vp_ref[...].T, vc_ref[...].T], axis=1)
    zeros = jnp.zeros((hd, lanes), BF16)
    ones = jnp.ones((2 * V7X_SUBLANES, KEY_SPAN), BF16)

    for h in range(N_KV_A):
        heads = [h * GROUP_A + g for g in range(GROUP_A)]
        qg = jnp.concatenate([q_t[hd_ * hd:(hd_ + 1) * hd, :] for hd_ in heads],
                             axis=1).astype(BF16)
        pair = h // 2
        rhs = jnp.concatenate([qg, zeros] if h % 2 == 0 else [zeros, qg], axis=0)
        logits = _dot(k_all[:, pair * 2 * hd:(pair + 1) * 2 * hd], rhs) + tab_ref[h]
        logits = jnp.concatenate([logits[0:HIST] + hist_pen, logits[HIST:KEY_SPAN]], axis=0)
        sink = jnp.concatenate([jnp.full((1, nq), sink_ref[hd_], F32) for hd_ in heads], axis=1)
        m = jnp.maximum(jnp.max(logits, axis=0, keepdims=True), sink)
        p = jnp.exp(logits - m).astype(BF16)
        v_aug = jnp.concatenate([v_t[h * hd:(h + 1) * hd, :].astype(BF16), ones], axis=0)
        acc = _dot(v_aug, p)
        denom = acc[hd:hd + 1, :] + jnp.exp(sink - m)
        out_t = acc[0:hd, :] * (1.0 / denom)
        for g in range(0, GROUP_A, 2):
            both = jnp.concatenate([out_t[:, g * nq:(g + 1) * nq],
                                    out_t[:, (g + 1) * nq:(g + 2) * nq]], axis=0)
            c0 = heads[g] * hd
            o_ref[:, c0:c0 + 2 * hd] = both.T.astype(o_ref.dtype)


def _swa(proj0, rel_bias, sinks, bucket):
    s = proj0.shape[0]
    nb = s // Q_BLOCK
    k_blk, v_blk = OFF_K // KV_COLS, OFF_V // KV_COLS
    kv_spec = lambda blk, prev: pl.BlockSpec(
        (Q_BLOCK, KV_COLS),
        (lambda i: (jnp.maximum(i - 1, 0), blk)) if prev else (lambda i: (i, blk)))
    return pl.pallas_call(
        _swa_body,
        grid=(nb,),
        in_specs=[pl.BlockSpec(memory_space=pltpu.SMEM),
                  pl.BlockSpec(memory_space=pltpu.SMEM),
                  pl.BlockSpec((KEY_SPAN, Q_BLOCK), lambda i: (0, 0)),
                  pl.BlockSpec((Q_BLOCK, Q_COLS), lambda i: (i, 0)),
                  kv_spec(k_blk, False), kv_spec(k_blk, True),
                  kv_spec(v_blk, False), kv_spec(v_blk, True)],
        out_specs=pl.BlockSpec((Q_BLOCK, Q_COLS), lambda i: (i, 0)),
        out_shape=jax.ShapeDtypeStruct((s, Q_COLS), BF16),
        scratch_shapes=[pltpu.VMEM((N_KV_A, KEY_SPAN, GROUP_A * Q_BLOCK), F32)],
        compiler_params=_params(1),
        name="swa_attention",
    )(rel_bias, sinks, bucket, proj0, proj0, proj0, proj0, proj0)


def _t5_bucket(rel):
    half = NUM_BUCKETS // 2
    ret = jnp.where(rel > 0, half, 0)
    n = jnp.abs(rel)
    max_exact = half // 2
    nf = jnp.maximum(n, 1).astype(F32)
    large = max_exact + (jnp.log(nf / max_exact) / math.log(MAX_DISTANCE / max_exact)
                         * (half - max_exact)).astype(jnp.int32)
    large = jnp.minimum(large, half - 1)
    return ret + jnp.where(n < max_exact, n, large)


def _swa_bucket_table():
    qi = jnp.arange(Q_BLOCK)
    sj = jnp.arange(KEY_SPAN)
    rel = (sj[:, None] - HIST) - qi[None, :]
    q_chunk = qi // CHUNK
    k_chunk = (sj - HIST) // CHUNK
    in_window = ((k_chunk[:, None] <= q_chunk[None, :])
                 & (k_chunk[:, None] >= q_chunk[None, :] - WINDOW_CHUNKS))
    return jnp.where(in_window, _t5_bucket(rel), -1).astype(jnp.int32)


LRU_PAD = V7X_SUBLANES
LRU_LANE_TILES = LRU_WIDTH // V7X_LANES
LRU_COL_BLOCKS = LRU_WIDTH // LRU_COL_BLOCK


def _neg_expm1(y):
    e = jnp.exp(y)
    safe_log = jnp.where(e == 1.0, 1.0, jnp.log(e))
    near_zero = jnp.where(e == 1.0, -y, (1.0 - e) * y / safe_log)
    return jnp.where(y > -0.5, near_zero, 1.0 - e)


def _lru_body(*refs):
    xb_refs = refs[:LRU_COL_BLOCKS]
    gb_refs = refs[LRU_COL_BLOCKS:2 * LRU_COL_BLOCKS]
    (cw_ref, cb_ref, wa_ref, ba_ref, wx_ref, bx_ref, lam_ref,
     o_ref, xpad_ref, a_ref, u_ref, hs_ref, hc_ref) = refs[2 * LRU_COL_BLOCKS:]
    t = pl.program_id(0)
    tb, cb = LRU_TIME_BLOCK, LRU_WIDTH

    @pl.when(t == 0)
    def _():
        xpad_ref[0:LRU_PAD, :] = jnp.zeros((LRU_PAD, cb), F32)
        hc_ref[...] = jnp.zeros((1, cb), F32)

    x = jnp.concatenate([r[...] for r in xb_refs], axis=1)
    gb = jnp.concatenate([r[...] for r in gb_refs], axis=1)
    xpad_ref[LRU_PAD:LRU_PAD + tb, :] = x
    xc = cb_ref[...] + cw_ref[CONV_WIDTH - 1:CONV_WIDTH, :] * x
    for w in range(CONV_WIDTH - 1):
        off = LRU_PAD - (CONV_WIDTH - 1) + w
        xc = xc + cw_ref[w:w + 1, :] * xpad_ref[off:off + tb, :]
    xpad_ref[0:LRU_PAD, :] = x[tb - LRU_PAD:tb, :]

    neg_c_softplus = -LRU_C * jax.nn.softplus(-lam_ref[...])
    for c in range(LRU_LANE_TILES):
        cs = slice(c * V7X_LANES, (c + 1) * V7X_LANES)
        xcc = xc[:, cs]
        xcb = xcc.astype(BF16)
        r = jax.nn.sigmoid(_dot(xcb, wa_ref[c]) + ba_ref[:, cs])
        ig = jax.nn.sigmoid(_dot(xcb, wx_ref[c]) + bx_ref[:, cs])
        log_a = r * neg_c_softplus[:, cs]
        a_ref[:, cs] = jnp.exp(log_a)
        u_ref[:, cs] = jnp.sqrt(_neg_expm1(2.0 * log_a)) * (ig * xcc)

    def scan_rows(r8, h):
        base = pl.multiple_of(r8 * V7X_SUBLANES, V7X_SUBLANES)
        for r in range(V7X_SUBLANES):
            h = a_ref[pl.ds(base + r, 1), :] * h + u_ref[pl.ds(base + r, 1), :]
            hs_ref[pl.ds(base + r, 1), :] = h
        return h

    hc_ref[...] = lax.fori_loop(0, tb // V7X_SUBLANES, scan_rows, hc_ref[...])
    o_ref[...] = (jax.nn.gelu(gb, approximate=True) * hs_ref[...]).astype(o_ref.dtype)


def _lru(proj0, conv_w, conv_b, wa_pairs, ba, wx_pairs, bx, lam):
    s = proj0.shape[0]
    tb, cb, wb = LRU_TIME_BLOCK, LRU_WIDTH, LRU_COL_BLOCK
    row = lambda v: v.reshape(1, cb)
    whole = lambda shape: pl.BlockSpec(shape, lambda t: (0,) * len(shape))
    col_blocks = lambda off: [
        pl.BlockSpec((tb, wb), functools.partial(lambda t, c: (t, c), c=off // wb + c))
        for c in range(LRU_COL_BLOCKS)]
    return pl.pallas_call(
        _lru_body,
        grid=(s // tb,),
        in_specs=col_blocks(OFF_X) + col_blocks(OFF_G) + [
            whole((CONV_WIDTH, cb)), whole((1, cb)),
            whole(wa_pairs.shape), whole((1, cb)),
            whole(wx_pairs.shape), whole((1, cb)), whole((1, cb))],
        out_specs=pl.BlockSpec((tb, cb), lambda t: (t, 0)),
        out_shape=jax.ShapeDtypeStruct((s, cb), BF16),
        scratch_shapes=[pltpu.VMEM((tb + LRU_PAD, cb), F32),
                        pltpu.VMEM((tb, cb), F32), pltpu.VMEM((tb, cb), F32),
                        pltpu.VMEM((tb, cb), F32), pltpu.VMEM((1, cb), F32)],
        compiler_params=_params(1),
        name="rg_lru",
    )(*([proj0] * (2 * LRU_COL_BLOCKS)), conv_w, row(conv_b), wa_pairs, row(ba),
      wx_pairs, row(bx), row(lam))


def _pair_block_diag(w):
    nb, bd, _ = w.shape
    wp = w.reshape(nb // 2, 2, bd, bd)
    z = jnp.zeros((nb // 2, bd, bd), w.dtype)
    top = jnp.concatenate([wp[:, 0], z], axis=2)
    bot = jnp.concatenate([z, wp[:, 1]], axis=2)
    return jnp.concatenate([top, bot], axis=1)


SB_LOG_F32_ZERO = -105.0


def _sb_scores(q, kb, tri2, mask):
    z = _dot_nt(q, kb) * (HEAD_DIM_C ** -0.5)
    sp = jnp.maximum(z, 0.0) + jnp.log(1.0 + jnp.exp(-jnp.abs(z)))
    if mask is not None:
        sp = jnp.where(mask, sp, 0.0)
    hi = sp.astype(BF16)
    lo = (sp - hi.astype(F32)).astype(BF16)
    return z, _dot(jnp.concatenate([hi, lo], axis=1), tri2)


def _sb_weighted(z, incl, carry, vb, mask):
    w = jnp.exp(z - (incl + carry))
    if mask is not None:
        w = jnp.where(mask, w, 0.0)
    return _dot(w.astype(BF16), vb)


def _sb_body(q_ref, k_ref, v_ref, tri_ref, o_ref):
    qi = pl.program_id(1)
    hd = HEAD_DIM_C
    heads = range(SB_HEADS_PER_STEP)
    qs = [q_ref[:, h * hd:(h + 1) * hd] for h in heads]
    tri2 = tri_ref[...]
    row_id = lax.broadcasted_iota(jnp.int32, (SB_TQ, SB_TK), 0)
    col_id = lax.broadcasted_iota(jnp.int32, (SB_TQ, SB_TK), 1)
    before = col_id < row_id

    def kv(j):
        start = pl.multiple_of(j * SB_TK, SB_TK)
        return ([k_ref[pl.ds(start, SB_TK), h * hd:(h + 1) * hd] for h in heads],
                [v_ref[pl.ds(start, SB_TK), h * hd:(h + 1) * hd] for h in heads])

    has_prev = qi > 0
    kd, vd = kv(qi)
    kp, vp = kv(jnp.maximum(qi - 1, 0))
    sd = [_sb_scores(qs[h], kd[h], tri2, before) for h in heads]
    sprev = [_sb_scores(qs[h], kp[h], tri2, None) for h in heads]
    state = ()
    for h in heads:
        (zd, incl_d), (zp, incl_p) = sd[h], sprev[h]
        carry = incl_d[:, 0:1]
        acc = _sb_weighted(zd, incl_d, 0.0, vd[h], before)
        acc_p = _sb_weighted(zp, incl_p, carry, vp[h], None)
        acc = acc + jnp.where(has_prev, acc_p, 0.0)
        carry = carry + jnp.where(has_prev, incl_p[:, 0:1], 0.0)
        state += (carry, acc)

    def live(loop_state):
        j, state = loop_state
        stick = functools.reduce(jnp.minimum, [jnp.min(state[2 * h]) for h in heads])
        return jnp.logical_and(j >= 0, -stick > SB_LOG_F32_ZERO)

    def body(loop_state):
        j, state = loop_state
        kb, vb = kv(j)
        scores = [_sb_scores(qs[h], kb[h], tri2, None) for h in heads]
        out = ()
        for h in heads:
            z, incl = scores[h]
            carry, acc = state[2 * h], state[2 * h + 1]
            out += (carry + incl[:, 0:1], acc + _sb_weighted(z, incl, carry, vb[h], None))
        return j - 1, out

    _, state = lax.while_loop(live, body, (qi - 2, state))
    for h in heads:
        o_ref[:, h * hd:(h + 1) * hd] = state[2 * h + 1].astype(o_ref.dtype)


def _stick_breaking(proj1):
    s = proj1.shape[0]
    hw = SB_HEADS_PER_STEP * HEAD_DIM_C
    n_groups = N_HEADS_C // SB_HEADS_PER_STEP
    jj = jnp.arange(SB_TK)
    tri = (jj[:, None] >= jj[None, :]).astype(BF16)
    tri2 = jnp.concatenate([tri, tri], axis=0)
    return pl.pallas_call(
        _sb_body,
        grid=(n_groups, s // SB_TQ),
        in_specs=[pl.BlockSpec((SB_TQ, hw), lambda g, i: (i, g)),
                  pl.BlockSpec((s, hw), lambda g, i: (0, n_groups + g)),
                  pl.BlockSpec((s, hw), lambda g, i: (0, 2 * n_groups + g)),
                  pl.BlockSpec((2 * SB_TK, SB_TK), lambda g, i: (0, 0))],
        out_specs=pl.BlockSpec((SB_TQ, hw), lambda g, i: (i, g)),
        out_shape=jax.ShapeDtypeStruct((s, N_HEADS_C * HEAD_DIM_C), BF16),
        compiler_params=_params(2),
        name="stick_breaking",
    )(proj1, proj1, proj1, tri2)


def kernel(x, rel_bias, norm_mix, even_w_in, even_conv_w, even_conv_b, even_gate_a_w,
           even_gate_a_b, even_gate_x_w, even_gate_x_b, even_lru_lambda, even_sinks,
           even_w_out, odd_w_in, odd_w_out, norm_mlp, w_up, w_down, final_norm):
    b, s, d = x.shape
    h = x.reshape(b * s, d)

    proj0 = _norm_matmul(h, norm_mix[0], even_w_in[0].astype(BF16), F32)
    y_a = _swa(proj0, rel_bias, even_sinks[0], _swa_bucket_table())
    y_b = _lru(proj0, even_conv_w[0], even_conv_b[0],
               _pair_block_diag(even_gate_a_w[0]).astype(BF16), even_gate_a_b[0],
               _pair_block_diag(even_gate_x_w[0]).astype(BF16), even_gate_x_b[0],
               even_lru_lambda[0])
    h = _matmul_res([y_a, y_b], even_w_out[0].astype(BF16), h)
    h = _mlp(h, norm_mlp[0], w_up, w_down, 0, None)

    proj1 = _norm_matmul(h, norm_mix[1], odd_w_in[0].astype(BF16), BF16)
    y = _stick_breaking(proj1)
    h = _matmul_res([y], odd_w_out[0].astype(BF16), h)
    h = _mlp(h, norm_mlp[1], w_up, w_down, 1, final_norm)
    return h.reshape(b, s, d)
```

```python
import functools
import math

import jax
import jax.numpy as jnp
from jax import lax
from jax.experimental import pallas as pl
from jax.experimental.pallas import tpu as pltpu

F32 = jnp.float32
BF16 = jnp.bfloat16

D_MODEL = 2048
CHUNK = 64
EPS = 1e-6

HEAD_DIM_A = 64
N_HEADS_A = 16
N_KV_A = 4
GROUP_A = N_HEADS_A // N_KV_A
WINDOW_CHUNKS = 2
Q_BLOCK = 128
HIST = WINDOW_CHUNKS * CHUNK
KEY_SPAN = Q_BLOCK + HIST
NUM_BUCKETS = 32
MAX_DISTANCE = 128

LRU_WIDTH = 1024
LRU_BLOCKS = 16
LRU_BLOCK_DIM = LRU_WIDTH // LRU_BLOCKS
CONV_WIDTH = 4
LRU_C = 8.0

HEAD_DIM_C = 128
N_HEADS_C = 16

Q_COLS = N_HEADS_A * HEAD_DIM_A
KV_COLS = N_KV_A * HEAD_DIM_A
OFF_K = Q_COLS
OFF_V = OFF_K + KV_COLS
OFF_X = OFF_V + KV_COLS
OFF_G = OFF_X + LRU_WIDTH

V7X_LANES = 128
V7X_SUBLANES = 8
V7X_VMEM_LIMIT_BYTES = 56 * 1024 * 1024

MASK_NEG = -1e30

PROJ_TM = 512
PROJ_TN_MAX = 1024
MLP_TM, MLP_TF = 1024, 512
LRU_TIME_BLOCK = 256
LRU_COL_BLOCK = 512
SB_TQ = 256
SB_HALF = 128
SB_PREV = 256
SB_HEADS_PER_STEP = 4


def _params(n_grid_dims):
    return pltpu.CompilerParams(
        dimension_semantics=("arbitrary",) * n_grid_dims,
        vmem_limit_bytes=V7X_VMEM_LIMIT_BYTES)


def _proj_tn(n):
    tn = PROJ_TN_MAX
    while n % tn:
        tn //= 2
    return tn


def _rms_norm_f32(x, g):
    ms = jnp.mean(x * x, axis=-1, keepdims=True)
    return (x * lax.rsqrt(ms + EPS)) * g


def _dot(a, b):
    return jnp.dot(a, b, preferred_element_type=F32)


def _dot_nt(a, b):
    return lax.dot_general(a, b, (((1,), (1,)), ((), ())), preferred_element_type=F32)


def _resident(shape):
    return pl.BlockSpec(shape, lambda i: (0,) * len(shape), pipeline_mode=pl.Buffered(1))


def _norm_matmul_body(x_ref, g_ref, w_ref, o_ref, hn_ref):
    hn_ref[...] = _rms_norm_f32(x_ref[...], g_ref[...]).astype(BF16)
    n = o_ref.shape[1]
    tn = _proj_tn(n)
    for c in range(n // tn):
        cs = slice(c * tn, (c + 1) * tn)
        o_ref[:, cs] = _dot(hn_ref[...], w_ref[:, cs]).astype(o_ref.dtype)


def _norm_matmul(x, g, w, out_dtype):
    m, k = x.shape
    n = w.shape[1]
    tm = PROJ_TM
    return pl.pallas_call(
        _norm_matmul_body,
        grid=(m // tm,),
        in_specs=[pl.BlockSpec((tm, k), lambda i: (i, 0)),
                  _resident((1, k)), _resident((k, n))],
        out_specs=pl.BlockSpec((tm, n), lambda i: (i, 0)),
        out_shape=jax.ShapeDtypeStruct((m, n), out_dtype),
        scratch_shapes=[pltpu.VMEM((tm, k), BF16)],
        compiler_params=_params(1),
        name="norm_matmul",
    )(x, g.reshape(1, k), w)


def _matmul_res_body(*refs, n_parts):
    a_refs = refs[:n_parts]
    w_ref, res_ref, o_ref = refs[n_parts:]
    kp = w_ref.shape[0] // n_parts
    n = o_ref.shape[1]
    tn = _proj_tn(n)
    for c in range(n // tn):
        cs = slice(c * tn, (c + 1) * tn)
        acc = res_ref[:, cs]
        for p, a_ref in enumerate(a_refs):
            acc = acc + _dot(a_ref[...], w_ref[p * kp:(p + 1) * kp, cs])
        o_ref[:, cs] = acc


def _matmul_res(a_list, w, res):
    m, n = res.shape
    tm = PROJ_TM
    n_parts = len(a_list)
    kp = w.shape[0] // n_parts
    in_specs = [pl.BlockSpec((tm, kp), lambda i: (i, 0)) for _ in a_list]
    in_specs += [_resident(w.shape), pl.BlockSpec((tm, n), lambda i: (i, 0))]
    return pl.pallas_call(
        functools.partial(_matmul_res_body, n_parts=n_parts),
        grid=(m // tm,),
        in_specs=in_specs,
        out_specs=pl.BlockSpec((tm, n), lambda i: (i, 0)),
        out_shape=jax.ShapeDtypeStruct((m, n), F32),
        compiler_params=_params(1),
        name="matmul_res",
    )(*a_list, w, res)


def _mlp_body(*refs, final_norm):
    if final_norm:
        h_ref, g_ref, wu_ref, wd_ref, fg_ref, o_ref, hn_ref = refs
    else:
        h_ref, g_ref, wu_ref, wd_ref, o_ref, hn_ref = refs
    f = pl.program_id(1)

    @pl.when(f == 0)
    def _():
        h = h_ref[...]
        hn_ref[...] = _rms_norm_f32(h, g_ref[...]).astype(BF16)
        o_ref[...] = h

    up = _dot(hn_ref[...], wu_ref[...].astype(BF16))
    act = jnp.square(jnp.maximum(up, 0.0)).astype(BF16)
    o_ref[...] += _dot(act, wd_ref[...].astype(BF16))

    if final_norm:
        @pl.when(f == pl.num_programs(1) - 1)
        def _():
            o_ref[...] = _rms_norm_f32(o_ref[...], fg_ref[...])


def _mlp(h, g, w_up, w_down, layer, final_g):
    m, d = h.shape
    ff = w_up.shape[2]
    tm, tf = MLP_TM, MLP_TF
    final_norm = final_g is not None
    in_specs = [pl.BlockSpec((tm, d), lambda i, f: (i, 0)),
                pl.BlockSpec((1, d), lambda i, f: (0, 0)),
                pl.BlockSpec((None, d, tf), lambda i, f: (layer, 0, f)),
                pl.BlockSpec((None, tf, d), lambda i, f: (layer, f, 0))]
    args = [h, g.reshape(1, d), w_up, w_down]
    if final_norm:
        in_specs.append(pl.BlockSpec((1, d), lambda i, f: (0, 0)))
        args.append(final_g.reshape(1, d))
    return pl.pallas_call(
        functools.partial(_mlp_body, final_norm=final_norm),
        grid=(m // tm, ff // tf),
        in_specs=in_specs,
        out_specs=pl.BlockSpec((tm, d), lambda i, f: (i, 0)),
        out_shape=jax.ShapeDtypeStruct((m, d), F32),
        scratch_shapes=[pltpu.VMEM((tm, d), BF16)],
        compiler_params=_params(2),
        name="mlp",
    )(*args)


def _swa_body(relb_ref, sink_ref, bucket_ref, q_ref, kc_ref, kp_ref, vc_ref, vp_ref,
              o_ref, tab_ref):
    i = pl.program_id(0)
    hd, nq = HEAD_DIM_A, Q_BLOCK
    lanes = GROUP_A * nq

    @pl.when(i == 0)
    def _():
        bucket = bucket_ref[...]
        for head in range(N_HEADS_A):
            t = jnp.full((KEY_SPAN, nq), MASK_NEG, F32)
            for b in range(NUM_BUCKETS):
                t = jnp.where(bucket == b, relb_ref[b, head], t)
            g = head % GROUP_A
            tab_ref[head // GROUP_A, :, g * nq:(g + 1) * nq] = t

    hist_pen = jnp.where(i == 0, MASK_NEG, 0.0).astype(F32)

    q_t = (q_ref[...] * (hd ** -0.5)).T
    k_all = jnp.concatenate([kp_ref[...], kc_ref[...]], axis=0).astype(BF16)
    v_t = jnp.concatenate([vp_ref[...].T, vc_ref[...].T], axis=1)
    zeros = jnp.zeros((hd, lanes), BF16)
    ones = jnp.ones((2 * V7X_SUBLANES, KEY_SPAN), BF16)

    kv_heads = range(N_KV_A)
    heads = [[h * GROUP_A + g for g in range(GROUP_A)] for h in kv_heads]
    all_logits = []
    for h in kv_heads:
        qg = jnp.concatenate([q_t[hd_ * hd:(hd_ + 1) * hd, :] for hd_ in heads[h]],
                             axis=1).astype(BF16)
        pair = h // 2
        rhs = jnp.concatenate([qg, zeros] if h % 2 == 0 else [zeros, qg], axis=0)
        all_logits.append(_dot(k_all[:, pair * 2 * hd:(pair + 1) * 2 * hd], rhs))
    sinks, maxes, accs = [], [], []
    for h in kv_heads:
        logits = all_logits[h] + tab_ref[h]
        logits = jnp.concatenate([logits[0:HIST] + hist_pen, logits[HIST:KEY_SPAN]], axis=0)
        sink = jnp.concatenate([jnp.full((1, nq), sink_ref[hd_], F32) for hd_ in heads[h]],
                               axis=1)
        m = jnp.maximum(jnp.max(logits, axis=0, keepdims=True), sink)
        p = jnp.exp(logits - m).astype(BF16)
        v_aug = jnp.concatenate([v_t[h * hd:(h + 1) * hd, :].astype(BF16), ones], axis=0)
        accs.append(_dot(v_aug, p))
        sinks.append(sink)
        maxes.append(m)
    for h in kv_heads:
        denom = accs[h][hd:hd + 1, :] + jnp.exp(sinks[h] - maxes[h])
        out_t = accs[h][0:hd, :] * (1.0 / denom)
        for g in range(0, GROUP_A, 2):
            both = jnp.concatenate([out_t[:, g * nq:(g + 1) * nq],
                                    out_t[:, (g + 1) * nq:(g + 2) * nq]], axis=0)
            c0 = heads[h][g] * hd
            o_ref[:, c0:c0 + 2 * hd] = both.T.astype(o_ref.dtype)


def _swa(proj0, rel_bias, sinks, bucket):
    s = proj0.shape[0]
    nb = s // Q_BLOCK
    k_blk, v_blk = OFF_K // KV_COLS, OFF_V // KV_COLS
    kv_spec = lambda blk, prev: pl.BlockSpec(
        (Q_BLOCK, KV_COLS),
        (lambda i: (jnp.maximum(i - 1, 0), blk)) if prev else (lambda i: (i, blk)))
    return pl.pallas_call(
        _swa_body,
        grid=(nb,),
        in_specs=[pl.BlockSpec(memory_space=pltpu.SMEM),
                  pl.BlockSpec(memory_space=pltpu.SMEM),
                  pl.BlockSpec((KEY_SPAN, Q_BLOCK), lambda i: (0, 0)),
                  pl.BlockSpec((Q_BLOCK, Q_COLS), lambda i: (i, 0)),
                  kv_spec(k_blk, False), kv_spec(k_blk, True),
                  kv_spec(v_blk, False), kv_spec(v_blk, True)],
        out_specs=pl.BlockSpec((Q_BLOCK, Q_COLS), lambda i: (i, 0)),
        out_shape=jax.ShapeDtypeStruct((s, Q_COLS), BF16),
        scratch_shapes=[pltpu.VMEM((N_KV_A, KEY_SPAN, GROUP_A * Q_BLOCK), F32)],
        compiler_params=_params(1),
        name="swa_attention",
    )(rel_bias, sinks, bucket, proj0, proj0, proj0, proj0, proj0)


def _t5_bucket(rel):
    half = NUM_BUCKETS // 2
    ret = jnp.where(rel > 0, half, 0)
    n = jnp.abs(rel)
    max_exact = half // 2
    nf = jnp.maximum(n, 1).astype(F32)
    large = max_exact + (jnp.log(nf / max_exact) / math.log(MAX_DISTANCE / max_exact)
                         * (half - max_exact)).astype(jnp.int32)
    large = jnp.minimum(large, half - 1)
    return ret + jnp.where(n < max_exact, n, large)


def _swa_bucket_table():
    qi = jnp.arange(Q_BLOCK)
    sj = jnp.arange(KEY_SPAN)
    rel = (sj[:, None] - HIST) - qi[None, :]
    q_chunk = qi // CHUNK
    k_chunk = (sj - HIST) // CHUNK
    in_window = ((k_chunk[:, None] <= q_chunk[None, :])
                 & (k_chunk[:, None] >= q_chunk[None, :] - WINDOW_CHUNKS))
    return jnp.where(in_window, _t5_bucket(rel), -1).astype(jnp.int32)


LRU_PAD = V7X_SUBLANES
LRU_LANE_TILES = LRU_WIDTH // V7X_LANES
LRU_COL_BLOCKS = LRU_WIDTH // LRU_COL_BLOCK


def _neg_expm1(y):
    e = jnp.exp(y)
    safe_log = jnp.where(e == 1.0, 1.0, jnp.log(e))
    near_zero = jnp.where(e == 1.0, -y, (1.0 - e) * y / safe_log)
    return jnp.where(y > -0.5, near_zero, 1.0 - e)


def _lru_body(*refs):
    xb_refs = refs[:LRU_COL_BLOCKS]
    gb_refs = refs[LRU_COL_BLOCKS:2 * LRU_COL_BLOCKS]
    (cw_ref, cb_ref, wa_ref, ba_ref, wx_ref, bx_ref, lam_ref,
     o_ref, xpad_ref, a_ref, u_ref, hs_ref, hc_ref) = refs[2 * LRU_COL_BLOCKS:]
    t = pl.program_id(0)
    tb, cb = LRU_TIME_BLOCK, LRU_WIDTH

    @pl.when(t == 0)
    def _():
        xpad_ref[0:LRU_PAD, :] = jnp.zeros((LRU_PAD, cb), F32)
        hc_ref[...] = jnp.zeros((1, cb), F32)

    x = jnp.concatenate([r[...] for r in xb_refs], axis=1)
    gb = jnp.concatenate([r[...] for r in gb_refs], axis=1)
    xpad_ref[LRU_PAD:LRU_PAD + tb, :] = x
    xc = cb_ref[...] + cw_ref[CONV_WIDTH - 1:CONV_WIDTH, :] * x
    for w in range(CONV_WIDTH - 1):
        off = LRU_PAD - (CONV_WIDTH - 1) + w
        xc = xc + cw_ref[w:w + 1, :] * xpad_ref[off:off + tb, :]
    xpad_ref[0:LRU_PAD, :] = x[tb - LRU_PAD:tb, :]

    neg_c_softplus = -LRU_C * jax.nn.softplus(-lam_ref[...])
    for c in range(LRU_LANE_TILES):
        cs = slice(c * V7X_LANES, (c + 1) * V7X_LANES)
        xcc = xc[:, cs]
        xcb = xcc.astype(BF16)
        r = jax.nn.sigmoid(_dot(xcb, wa_ref[c]) + ba_ref[:, cs])
        ig = jax.nn.sigmoid(_dot(xcb, wx_ref[c]) + bx_ref[:, cs])
        log_a = r * neg_c_softplus[:, cs]
        a_ref[:, cs] = jnp.exp(log_a)
        u_ref[:, cs] = jnp.sqrt(_neg_expm1(2.0 * log_a)) * (ig * xcc)

    def scan_rows(r8, h):
        base = pl.multiple_of(r8 * V7X_SUBLANES, V7X_SUBLANES)
        for r in range(V7X_SUBLANES):
            h = a_ref[pl.ds(base + r, 1), :] * h + u_ref[pl.ds(base + r, 1), :]
            hs_ref[pl.ds(base + r, 1), :] = h
        return h

    hc_ref[...] = lax.fori_loop(0, tb // V7X_SUBLANES, scan_rows, hc_ref[...])
    o_ref[...] = (jax.nn.gelu(gb, approximate=True) * hs_ref[...]).astype(o_ref.dtype)


def _lru(proj0, conv_w, conv_b, wa_pairs, ba, wx_pairs, bx, lam):
    s = proj0.shape[0]
    tb, cb, wb = LRU_TIME_BLOCK, LRU_WIDTH, LRU_COL_BLOCK
    row = lambda v: v.reshape(1, cb)
    whole = lambda shape: pl.BlockSpec(shape, lambda t: (0,) * len(shape))
    col_blocks = lambda off: [
        pl.BlockSpec((tb, wb), functools.partial(lambda t, c: (t, c), c=off // wb + c))
        for c in range(LRU_COL_BLOCKS)]
    return pl.pallas_call(
        _lru_body,
        grid=(s // tb,),
        in_specs=col_blocks(OFF_X) + col_blocks(OFF_G) + [
            whole((CONV_WIDTH, cb)), whole((1, cb)),
            whole(wa_pairs.shape), whole((1, cb)),
            whole(wx_pairs.shape), whole((1, cb)), whole((1, cb))],
        out_specs=pl.BlockSpec((tb, cb), lambda t: (t, 0)),
        out_shape=jax.ShapeDtypeStruct((s, cb), BF16),
        scratch_shapes=[pltpu.VMEM((tb + LRU_PAD, cb), F32),
                        pltpu.VMEM((tb, cb), F32), pltpu.VMEM((tb, cb), F32),
                        pltpu.VMEM((tb, cb), F32), pltpu.VMEM((1, cb), F32)],
        compiler_params=_params(1),
        name="rg_lru",
    )(*([proj0] * (2 * LRU_COL_BLOCKS)), conv_w, row(conv_b), wa_pairs, row(ba),
      wx_pairs, row(bx), row(lam))


def _pair_block_diag(w):
    nb, bd, _ = w.shape
    wp = w.reshape(nb // 2, 2, bd, bd)
    z = jnp.zeros((nb // 2, bd, bd), w.dtype)
    top = jnp.concatenate([wp[:, 0], z], axis=2)
    bot = jnp.concatenate([z, wp[:, 1]], axis=2)
    return jnp.concatenate([top, bot], axis=1)


SB_LOG_F32_ZERO = -105.0


def _sb_logits(q, kb, mask):
    z = _dot_nt(q, kb) * (HEAD_DIM_C ** -0.5)
    sp = jnp.maximum(z, 0.0) + jnp.log(1.0 + jnp.exp(-jnp.abs(z)))
    if mask is not None:
        sp = jnp.where(mask, sp, 0.0)
    hi = sp.astype(BF16)
    lo = (sp - hi.astype(F32)).astype(BF16)
    return z, jnp.concatenate([hi, lo], axis=1)


def _sb_suffix_sums(hi_lo_list, tri2):
    rows = hi_lo_list[0].shape[0]
    out = _dot(jnp.concatenate(hi_lo_list, axis=0), tri2)
    return [out[t * rows:(t + 1) * rows] for t in range(len(hi_lo_list))]


def _sb_scores(q, kb, tri2, mask):
    z, hi_lo = _sb_logits(q, kb, mask)
    return z, _sb_suffix_sums([hi_lo], tri2)[0]


def _sb_weighted(z, incl, carry, vb, mask):
    w = jnp.exp(z - (incl + carry))
    if mask is not None:
        w = jnp.where(mask, w, 0.0)
    return _dot(w.astype(BF16), vb)


def _sb_body(q_ref, k_ref, v_ref, tri_prev_ref, tri_half_ref, o_ref):
    qi = pl.program_id(1)
    hd, nh = HEAD_DIM_C, SB_HALF
    heads = range(SB_HEADS_PER_STEP)
    tri_prev = tri_prev_ref[...]
    tri_half = tri_half_ref[...]
    row_id = lax.broadcasted_iota(jnp.int32, (nh, nh), 0)
    col_id = lax.broadcasted_iota(jnp.int32, (nh, nh), 1)
    before = col_id < row_id
    cols = lambda h: slice(h * hd, (h + 1) * hd)
    rows = lambda half: slice(half * nh, (half + 1) * nh)

    def kv(start, size, h):
        start = pl.multiple_of(start, nh)
        return k_ref[pl.ds(start, size), cols(h)], v_ref[pl.ds(start, size), cols(h)]

    @pl.when(qi == 0)
    def _first_block():
        for h in heads:
            k0, v0 = kv(0, nh, h)
            k1, v1 = kv(nh, nh, h)
            z, incl = _sb_scores(q_ref[rows(0), cols(h)], k0, tri_half, before)
            o_ref[rows(0), cols(h)] = _sb_weighted(z, incl, 0.0, v0, before).astype(o_ref.dtype)
            q_bot = q_ref[rows(1), cols(h)]
            zd, incl_d = _sb_scores(q_bot, k1, tri_half, before)
            zp, incl_p = _sb_scores(q_bot, k0, tri_half, None)
            acc = (_sb_weighted(zd, incl_d, 0.0, v1, before)
                   + _sb_weighted(zp, incl_p, incl_d[:, 0:1], v0, None))
            o_ref[rows(1), cols(h)] = acc.astype(o_ref.dtype)

    @pl.when(qi > 0)
    def _later_blocks():
        base = qi * SB_TQ
        tiles = [(h, half) for h in heads for half in range(SB_TQ // nh)]
        window_start = lambda half: base - SB_PREV + half * nh
        qs = [q_ref[rows(half), cols(h)] for h, half in tiles]
        kv_prev = [kv(window_start(half), SB_PREV, h) for h, half in tiles]
        kv_diag = [kv(window_start(half) + SB_PREV, nh, h) for h, half in tiles]
        l_diag = [_sb_logits(qs[t], kv_diag[t][0], before) for t in range(len(tiles))]
        l_prev = [_sb_logits(qs[t], kv_prev[t][0], None) for t in range(len(tiles))]
        i_diag = _sb_suffix_sums([hl for _, hl in l_diag], tri_half)
        i_prev = _sb_suffix_sums([hl for _, hl in l_prev], tri_prev)
        state = ()
        for t in range(len(tiles)):
            zd, incl_d, zp, incl_p = l_diag[t][0], i_diag[t], l_prev[t][0], i_prev[t]
            carry = incl_d[:, 0:1]
            acc = (_sb_weighted(zd, incl_d, 0.0, kv_diag[t][1], before)
                   + _sb_weighted(zp, incl_p, carry, kv_prev[t][1], None))
            state += (carry + incl_p[:, 0:1], acc)

        def live(loop_state):
            n, state = loop_state
            stick = functools.reduce(jnp.minimum, [jnp.min(c) for c in state[0::2]])
            more_keys = window_start(1) - (n + 1) * nh >= 0
            return jnp.logical_and(more_keys, -stick > SB_LOG_F32_ZERO)

        def body(loop_state):
            n, state = loop_state
            starts = [window_start(half) - (n + 1) * nh for _, half in tiles]
            kvs = [kv(jnp.maximum(starts[t], 0), nh, h) for t, (h, _) in enumerate(tiles)]
            logits = [_sb_logits(qs[t], kvs[t][0], None) for t in range(len(tiles))]
            incls = _sb_suffix_sums([hl for _, hl in logits], tri_half)
            out = ()
            for t in range(len(tiles)):
                valid = starts[t] >= 0
                carry, acc = state[2 * t], state[2 * t + 1]
                add = _sb_weighted(logits[t][0], incls[t], carry, kvs[t][1], None)
                out += (carry + jnp.where(valid, incls[t][:, 0:1], 0.0),
                        acc + jnp.where(valid, add, 0.0))
            return n + 1, out

        _, state = lax.while_loop(live, body, (jnp.int32(0), state))
        for t, (h, half) in enumerate(tiles):
            o_ref[rows(half), cols(h)] = state[2 * t + 1].astype(o_ref.dtype)


def _suffix_sum_matrix(n):
    jj = jnp.arange(n)
    tri = (jj[:, None] >= jj[None, :]).astype(BF16)
    return jnp.concatenate([tri, tri], axis=0)


def _stick_breaking(proj1):
    s = proj1.shape[0]
    hw = SB_HEADS_PER_STEP * HEAD_DIM_C
    n_groups = N_HEADS_C // SB_HEADS_PER_STEP
    whole = lambda shape: pl.BlockSpec(shape, lambda g, i: (0,) * len(shape))
    return pl.pallas_call(
        _sb_body,
        grid=(n_groups, s // SB_TQ),
        in_specs=[pl.BlockSpec((SB_TQ, hw), lambda g, i: (i, g)),
                  pl.BlockSpec((s, hw), lambda g, i: (0, n_groups + g)),
                  pl.BlockSpec((s, hw), lambda g, i: (0, 2 * n_groups + g)),
                  whole((2 * SB_PREV, SB_PREV)), whole((2 * SB_HALF, SB_HALF))],
        out_specs=pl.BlockSpec((SB_TQ, hw), lambda g, i: (i, g)),
        out_shape=jax.ShapeDtypeStruct((s, N_HEADS_C * HEAD_DIM_C), BF16),
        compiler_params=_params(2),
        name="stick_breaking",
    )(proj1, proj1, proj1, _suffix_sum_matrix(SB_PREV), _suffix_sum_matrix(SB_HALF))


def kernel(x, rel_bias, norm_mix, even_w_in, even_conv_w, even_conv_b, even_gate_a_w,
           even_gate_a_b, even_gate_x_w, even_gate_x_b, even_lru_lambda, even_sinks,
           even_w_out, odd_w_in, odd_w_out, norm_mlp, w_up, w_down, final_norm):
    b, s, d = x.shape
    h = x.reshape(b * s, d)

    proj0 = _norm_matmul(h, norm_mix[0], even_w_in[0].astype(BF16), F32)
    y_a = _swa(proj0, rel_bias, even_sinks[0], _swa_bucket_table())
    y_b = _lru(proj0, even_conv_w[0], even_conv_b[0],
               _pair_block_diag(even_gate_a_w[0]).astype(BF16), even_gate_a_b[0],
               _pair_block_diag(even_gate_x_w[0]).astype(BF16), even_gate_x_b[0],
               even_lru_lambda[0])
    h = _matmul_res([y_a, y_b], even_w_out[0].astype(BF16), h)
    h = _mlp(h, norm_mlp[0], w_up, w_down, 0, None)

    proj1 = _norm_matmul(h, norm_mix[1], odd_w_in[0].astype(BF16), BF16)
    y = _stick_breaking(proj1)
    h = _matmul_res([y], odd_w_out[0].astype(BF16), h)
    h = _mlp(h, norm_mlp[1], w_up, w_down, 1, final_norm)
    return h.reshape(b, s, d)
```

```python
import functools
import math

import jax
import jax.numpy as jnp
from jax import lax
from jax.experimental import pallas as pl
from jax.experimental.pallas import tpu as pltpu

F32 = jnp.float32
BF16 = jnp.bfloat16

D_MODEL = 2048
CHUNK = 64
EPS = 1e-6

HEAD_DIM_A = 64
N_HEADS_A = 16
N_KV_A = 4
GROUP_A = N_HEADS_A // N_KV_A
WINDOW_CHUNKS = 2
Q_BLOCK = 128
HIST = WINDOW_CHUNKS * CHUNK
KEY_SPAN = Q_BLOCK + HIST
NUM_BUCKETS = 32
MAX_DISTANCE = 128

LRU_WIDTH = 1024
LRU_BLOCKS = 16
LRU_BLOCK_DIM = LRU_WIDTH // LRU_BLOCKS
CONV_WIDTH = 4
LRU_C = 8.0

HEAD_DIM_C = 128
N_HEADS_C = 16

Q_COLS = N_HEADS_A * HEAD_DIM_A
KV_COLS = N_KV_A * HEAD_DIM_A
OFF_K = Q_COLS
OFF_V = OFF_K + KV_COLS
OFF_X = OFF_V + KV_COLS
OFF_G = OFF_X + LRU_WIDTH

V7X_LANES = 128
V7X_SUBLANES = 8
V7X_VMEM_LIMIT_BYTES = 56 * 1024 * 1024

MASK_NEG = -1e30

PROJ_TM = 512
PROJ_TN_MAX = 1024
PROJ_LOAD_COLS = 256
MLP_TM, MLP_TF = 1024, 512
LRU_TIME_BLOCK = 256
LRU_COL_BLOCK = 512
SB_TQ = 256
SB_HALF = 128
SB_PREV = 256
SB_HEADS_PER_STEP = 4


def _params(n_grid_dims):
    return pltpu.CompilerParams(
        dimension_semantics=("arbitrary",) * n_grid_dims,
        vmem_limit_bytes=V7X_VMEM_LIMIT_BYTES)


def _proj_tn(n):
    tn = PROJ_TN_MAX
    while n % tn:
        tn //= 2
    return tn


def _rms_norm_f32(x, g):
    ms = jnp.mean(x * x, axis=-1, keepdims=True)
    return (x * lax.rsqrt(ms + EPS)) * g


def _dot(a, b):
    return jnp.dot(a, b, preferred_element_type=F32)


def _dot_nt(a, b):
    return lax.dot_general(a, b, (((1,), (1,)), ((), ())), preferred_element_type=F32)


def _resident(shape):
    return pl.BlockSpec(shape, lambda i: (0,) * len(shape), pipeline_mode=pl.Buffered(1))


def _weight_scratch(k, n):
    return [pltpu.VMEM((k, n), BF16),
            pltpu.VMEM((2, k, PROJ_LOAD_COLS), F32),
            pltpu.SemaphoreType.DMA((2,))]


def _for_each_column_chunk(w_hbm, w_ref, stage_ref, sem, tn, compute):
    n = w_ref.shape[1]
    lc = PROJ_LOAD_COLS
    n_loads, loads_per_chunk = n // lc, tn // lc

    def copy(c):
        return pltpu.make_async_copy(w_hbm.at[:, pl.ds(c * lc, lc)], stage_ref.at[c % 2],
                                     sem.at[c % 2])

    @pl.when(pl.program_id(0) == 0)
    def _():
        copy(0).start()
        for c in range(n // tn):
            for l in range(c * loads_per_chunk, (c + 1) * loads_per_chunk):
                if l + 1 < n_loads:
                    copy(l + 1).start()
                copy(l).wait()
                w_ref[:, l * lc:(l + 1) * lc] = stage_ref[l % 2].astype(BF16)
            compute(slice(c * tn, (c + 1) * tn))

    @pl.when(pl.program_id(0) > 0)
    def _():
        for c in range(n // tn):
            compute(slice(c * tn, (c + 1) * tn))


def _norm_matmul_body(x_ref, g_ref, w_hbm, o_ref, hn_ref, w_ref, stage_ref, sem):
    hn_ref[...] = _rms_norm_f32(x_ref[...], g_ref[...]).astype(BF16)

    def compute(cs):
        o_ref[:, cs] = _dot(hn_ref[...], w_ref[:, cs]).astype(o_ref.dtype)

    _for_each_column_chunk(w_hbm, w_ref, stage_ref, sem, _proj_tn(o_ref.shape[1]), compute)


def _norm_matmul(x, g, w, out_dtype):
    m, k = x.shape
    n = w.shape[1]
    tm = PROJ_TM
    return pl.pallas_call(
        _norm_matmul_body,
        grid=(m // tm,),
        in_specs=[pl.BlockSpec((tm, k), lambda i: (i, 0)),
                  _resident((1, k)), pl.BlockSpec(memory_space=pl.ANY)],
        out_specs=pl.BlockSpec((tm, n), lambda i: (i, 0)),
        out_shape=jax.ShapeDtypeStruct((m, n), out_dtype),
        scratch_shapes=[pltpu.VMEM((tm, k), BF16)] + _weight_scratch(k, n),
        compiler_params=_params(1),
        name="norm_matmul",
    )(x, g.reshape(1, k), w)


def _matmul_res_body(*refs, n_parts):
    a_refs = refs[:n_parts]
    w_hbm, res_ref, o_ref, w_ref, stage_ref, sem = refs[n_parts:]
    kp = w_ref.shape[0] // n_parts

    def compute(cs):
        acc = res_ref[:, cs]
        for p, a_ref in enumerate(a_refs):
            acc = acc + _dot(a_ref[...], w_ref[p * kp:(p + 1) * kp, cs])
        o_ref[:, cs] = acc

    _for_each_column_chunk(w_hbm, w_ref, stage_ref, sem, _proj_tn(o_ref.shape[1]), compute)


def _matmul_res(a_list, w, res):
    m, n = res.shape
    tm = PROJ_TM
    n_parts = len(a_list)
    kp = w.shape[0] // n_parts
    in_specs = [pl.BlockSpec((tm, kp), lambda i: (i, 0)) for _ in a_list]
    in_specs += [pl.BlockSpec(memory_space=pl.ANY), pl.BlockSpec((tm, n), lambda i: (i, 0))]
    return pl.pallas_call(
        functools.partial(_matmul_res_body, n_parts=n_parts),
        grid=(m // tm,),
        in_specs=in_specs,
        out_specs=pl.BlockSpec((tm, n), lambda i: (i, 0)),
        out_shape=jax.ShapeDtypeStruct((m, n), F32),
        scratch_shapes=_weight_scratch(w.shape[0], n),
        compiler_params=_params(1),
        name="matmul_res",
    )(*a_list, w, res)


def _mlp_body(*refs, final_norm):
    if final_norm:
        h_ref, g_ref, wu_ref, wd_ref, fg_ref, o_ref, hn_ref = refs
    else:
        h_ref, g_ref, wu_ref, wd_ref, o_ref, hn_ref = refs
    f = pl.program_id(1)

    @pl.when(f == 0)
    def _():
        h = h_ref[...]
        hn_ref[...] = _rms_norm_f32(h, g_ref[...]).astype(BF16)
        o_ref[...] = h

    up = _dot(hn_ref[...], wu_ref[...].astype(BF16))
    act = jnp.square(jnp.maximum(up, 0.0)).astype(BF16)
    o_ref[...] += _dot(act, wd_ref[...].astype(BF16))

    if final_norm:
        @pl.when(f == pl.num_programs(1) - 1)
        def _():
            o_ref[...] = _rms_norm_f32(o_ref[...], fg_ref[...])


def _mlp(h, g, w_up, w_down, layer, final_g):
    m, d = h.shape
    ff = w_up.shape[2]
    tm, tf = MLP_TM, MLP_TF
    final_norm = final_g is not None
    in_specs = [pl.BlockSpec((tm, d), lambda i, f: (i, 0)),
                pl.BlockSpec((1, d), lambda i, f: (0, 0)),
                pl.BlockSpec((None, d, tf), lambda i, f: (layer, 0, f)),
                pl.BlockSpec((None, tf, d), lambda i, f: (layer, f, 0))]
    args = [h, g.reshape(1, d), w_up, w_down]
    if final_norm:
        in_specs.append(pl.BlockSpec((1, d), lambda i, f: (0, 0)))
        args.append(final_g.reshape(1, d))
    return pl.pallas_call(
        functools.partial(_mlp_body, final_norm=final_norm),
        grid=(m // tm, ff // tf),
        in_specs=in_specs,
        out_specs=pl.BlockSpec((tm, d), lambda i, f: (i, 0)),
        out_shape=jax.ShapeDtypeStruct((m, d), F32),
        scratch_shapes=[pltpu.VMEM((tm, d), BF16)],
        compiler_params=_params(2),
        name="mlp",
    )(*args)


def _swa_body(relb_ref, sink_ref, bucket_ref, q_ref, kc_ref, kp_ref, vc_ref, vp_ref,
              o_ref, tab_ref):
    i = pl.program_id(0)
    hd, nq = HEAD_DIM_A, Q_BLOCK
    lanes = GROUP_A * nq

    @pl.when(i == 0)
    def _():
        bucket = bucket_ref[...]
        for head in range(N_HEADS_A):
            t = jnp.full((KEY_SPAN, nq), MASK_NEG, F32)
            for b in range(NUM_BUCKETS):
                t = jnp.where(bucket == b, relb_ref[b, head], t)
            g = head % GROUP_A
            tab_ref[head // GROUP_A, :, g * nq:(g + 1) * nq] = t

    hist_pen = jnp.where(i == 0, MASK_NEG, 0.0).astype(F32)

    q_t = (q_ref[...] * (hd ** -0.5)).T
    k_all = jnp.concatenate([kp_ref[...], kc_ref[...]], axis=0).astype(BF16)
    v_t = jnp.concatenate([vp_ref[...].T, vc_ref[...].T], axis=1)
    zeros = jnp.zeros((hd, lanes), BF16)
    ones = jnp.ones((2 * V7X_SUBLANES, KEY_SPAN), BF16)

    kv_heads = range(N_KV_A)
    heads = [[h * GROUP_A + g for g in range(GROUP_A)] for h in kv_heads]
    all_logits = []
    for h in kv_heads:
        qg = jnp.concatenate([q_t[hd_ * hd:(hd_ + 1) * hd, :] for hd_ in heads[h]],
                             axis=1).astype(BF16)
        pair = h // 2
        rhs = jnp.concatenate([qg, zeros] if h % 2 == 0 else [zeros, qg], axis=0)
        all_logits.append(_dot(k_all[:, pair * 2 * hd:(pair + 1) * 2 * hd], rhs))
    sinks, maxes, accs = [], [], []
    for h in kv_heads:
        logits = all_logits[h] + tab_ref[h]
        logits = jnp.concatenate([logits[0:HIST] + hist_pen, logits[HIST:KEY_SPAN]], axis=0)
        sink = jnp.concatenate([jnp.full((1, nq), sink_ref[hd_], F32) for hd_ in heads[h]],
                               axis=1)
        m = jnp.maximum(jnp.max(logits, axis=0, keepdims=True), sink)
        p = jnp.exp(logits - m).astype(BF16)
        v_aug = jnp.concatenate([v_t[h * hd:(h + 1) * hd, :].astype(BF16), ones], axis=0)
        accs.append(_dot(v_aug, p))
        sinks.append(sink)
        maxes.append(m)
    for h in kv_heads:
        denom = accs[h][hd:hd + 1, :] + jnp.exp(sinks[h] - maxes[h])
        out_t = accs[h][0:hd, :] * (1.0 / denom)
        for g in range(0, GROUP_A, 2):
            both = jnp.concatenate([out_t[:, g * nq:(g + 1) * nq],
                                    out_t[:, (g + 1) * nq:(g + 2) * nq]], axis=0)
            c0 = heads[h][g] * hd
            o_ref[:, c0:c0 + 2 * hd] = both.T.astype(o_ref.dtype)


def _swa(proj0, rel_bias, sinks, bucket):
    s = proj0.shape[0]
    nb = s // Q_BLOCK
    k_blk, v_blk = OFF_K // KV_COLS, OFF_V // KV_COLS
    kv_spec = lambda blk, prev: pl.BlockSpec(
        (Q_BLOCK, KV_COLS),
        (lambda i: (jnp.maximum(i - 1, 0), blk)) if prev else (lambda i: (i, blk)))
    return pl.pallas_call(
        _swa_body,
        grid=(nb,),
        in_specs=[pl.BlockSpec(memory_space=pltpu.SMEM),
                  pl.BlockSpec(memory_space=pltpu.SMEM),
                  pl.BlockSpec((KEY_SPAN, Q_BLOCK), lambda i: (0, 0)),
                  pl.BlockSpec((Q_BLOCK, Q_COLS), lambda i: (i, 0)),
                  kv_spec(k_blk, False), kv_spec(k_blk, True),
                  kv_spec(v_blk, False), kv_spec(v_blk, True)],
        out_specs=pl.BlockSpec((Q_BLOCK, Q_COLS), lambda i: (i, 0)),
        out_shape=jax.ShapeDtypeStruct((s, Q_COLS), BF16),
        scratch_shapes=[pltpu.VMEM((N_KV_A, KEY_SPAN, GROUP_A * Q_BLOCK), F32)],
        compiler_params=_params(1),
        name="swa_attention",
    )(rel_bias, sinks, bucket, proj0, proj0, proj0, proj0, proj0)


def _t5_bucket(rel):
    half = NUM_BUCKETS // 2
    ret = jnp.where(rel > 0, half, 0)
    n = jnp.abs(rel)
    max_exact = half // 2
    nf = jnp.maximum(n, 1).astype(F32)
    large = max_exact + (jnp.log(nf / max_exact) / math.log(MAX_DISTANCE / max_exact)
                         * (half - max_exact)).astype(jnp.int32)
    large = jnp.minimum(large, half - 1)
    return ret + jnp.where(n < max_exact, n, large)


def _swa_bucket_table():
    qi = jnp.arange(Q_BLOCK)
    sj = jnp.arange(KEY_SPAN)
    rel = (sj[:, None] - HIST) - qi[None, :]
    q_chunk = qi // CHUNK
    k_chunk = (sj - HIST) // CHUNK
    in_window = ((k_chunk[:, None] <= q_chunk[None, :])
                 & (k_chunk[:, None] >= q_chunk[None, :] - WINDOW_CHUNKS))
    return jnp.where(in_window, _t5_bucket(rel), -1).astype(jnp.int32)


LRU_PAD = V7X_SUBLANES
LRU_LANE_TILES = LRU_WIDTH // V7X_LANES
LRU_COL_BLOCKS = LRU_WIDTH // LRU_COL_BLOCK


def _neg_expm1(y):
    e = jnp.exp(y)
    safe_log = jnp.where(e == 1.0, 1.0, jnp.log(e))
    near_zero = jnp.where(e == 1.0, -y, (1.0 - e) * y / safe_log)
    return jnp.where(y > -0.5, near_zero, 1.0 - e)


def _lru_body(*refs):
    xb_refs = refs[:LRU_COL_BLOCKS]
    gb_refs = refs[LRU_COL_BLOCKS:2 * LRU_COL_BLOCKS]
    (cw_ref, cb_ref, wa_ref, ba_ref, wx_ref, bx_ref, lam_ref,
     o_ref, xpad_ref, a_ref, u_ref, hs_ref, hc_ref) = refs[2 * LRU_COL_BLOCKS:]
    t = pl.program_id(0)
    tb, cb = LRU_TIME_BLOCK, LRU_WIDTH

    @pl.when(t == 0)
    def _():
        xpad_ref[0:LRU_PAD, :] = jnp.zeros((LRU_PAD, cb), F32)
        hc_ref[...] = jnp.zeros((1, cb), F32)

    x = jnp.concatenate([r[...] for r in xb_refs], axis=1)
    gb = jnp.concatenate([r[...] for r in gb_refs], axis=1)
    xpad_ref[LRU_PAD:LRU_PAD + tb, :] = x
    xc = cb_ref[...] + cw_ref[CONV_WIDTH - 1:CONV_WIDTH, :] * x
    for w in range(CONV_WIDTH - 1):
        off = LRU_PAD - (CONV_WIDTH - 1) + w
        xc = xc + cw_ref[w:w + 1, :] * xpad_ref[off:off + tb, :]
    xpad_ref[0:LRU_PAD, :] = x[tb - LRU_PAD:tb, :]

    neg_c_softplus = -LRU_C * jax.nn.softplus(-lam_ref[...])
    for c in range(LRU_LANE_TILES):
        cs = slice(c * V7X_LANES, (c + 1) * V7X_LANES)
        xcc = xc[:, cs]
        xcb = xcc.astype(BF16)
        r = jax.nn.sigmoid(_dot(xcb, wa_ref[c]) + ba_ref[:, cs])
        ig = jax.nn.sigmoid(_dot(xcb, wx_ref[c]) + bx_ref[:, cs])
        log_a = r * neg_c_softplus[:, cs]
        a_ref[:, cs] = jnp.exp(log_a)
        u_ref[:, cs] = jnp.sqrt(_neg_expm1(2.0 * log_a)) * (ig * xcc)

    def scan_rows(r8, h):
        base = pl.multiple_of(r8 * V7X_SUBLANES, V7X_SUBLANES)
        for r in range(V7X_SUBLANES):
            h = a_ref[pl.ds(base + r, 1), :] * h + u_ref[pl.ds(base + r, 1), :]
            hs_ref[pl.ds(base + r, 1), :] = h
        return h

    hc_ref[...] = lax.fori_loop(0, tb // V7X_SUBLANES, scan_rows, hc_ref[...])
    o_ref[...] = (jax.nn.gelu(gb, approximate=True) * hs_ref[...]).astype(o_ref.dtype)


def _lru(proj0, conv_w, conv_b, wa_pairs, ba, wx_pairs, bx, lam):
    s = proj0.shape[0]
    tb, cb, wb = LRU_TIME_BLOCK, LRU_WIDTH, LRU_COL_BLOCK
    row = lambda v: v.reshape(1, cb)
    whole = lambda shape: pl.BlockSpec(shape, lambda t: (0,) * len(shape))
    col_blocks = lambda off: [
        pl.BlockSpec((tb, wb), functools.partial(lambda t, c: (t, c), c=off // wb + c))
        for c in range(LRU_COL_BLOCKS)]
    return pl.pallas_call(
        _lru_body,
        grid=(s // tb,),
        in_specs=col_blocks(OFF_X) + col_blocks(OFF_G) + [
            whole((CONV_WIDTH, cb)), whole((1, cb)),
            whole(wa_pairs.shape), whole((1, cb)),
            whole(wx_pairs.shape), whole((1, cb)), whole((1, cb))],
        out_specs=pl.BlockSpec((tb, cb), lambda t: (t, 0)),
        out_shape=jax.ShapeDtypeStruct((s, cb), BF16),
        scratch_shapes=[pltpu.VMEM((tb + LRU_PAD, cb), F32),
                        pltpu.VMEM((tb, cb), F32), pltpu.VMEM((tb, cb), F32),
                        pltpu.VMEM((tb, cb), F32), pltpu.VMEM((1, cb), F32)],
        compiler_params=_params(1),
        name="rg_lru",
    )(*([proj0] * (2 * LRU_COL_BLOCKS)), conv_w, row(conv_b), wa_pairs, row(ba),
      wx_pairs, row(bx), row(lam))


def _pair_block_diag(w):
    nb, bd, _ = w.shape
    wp = w.reshape(nb // 2, 2, bd, bd)
    z = jnp.zeros((nb // 2, bd, bd), w.dtype)
    top = jnp.concatenate([wp[:, 0], z], axis=2)
    bot = jnp.concatenate([z, wp[:, 1]], axis=2)
    return jnp.concatenate([top, bot], axis=1)


SB_LOG_F32_ZERO = -105.0


def _sb_logits(q, kb, mask):
    z = _dot_nt(q, kb) * (HEAD_DIM_C ** -0.5)
    sp = jnp.maximum(z, 0.0) + jnp.log(1.0 + jnp.exp(-jnp.abs(z)))
    if mask is not None:
        sp = jnp.where(mask, sp, 0.0)
    hi = sp.astype(BF16)
    lo = (sp - hi.astype(F32)).astype(BF16)
    return z, jnp.concatenate([hi, lo], axis=1)


def _sb_suffix_sums(hi_lo_list, tri2):
    rows = hi_lo_list[0].shape[0]
    out = _dot(jnp.concatenate(hi_lo_list, axis=0), tri2)
    return [out[t * rows:(t + 1) * rows] for t in range(len(hi_lo_list))]


def _sb_scores(q, kb, tri2, mask):
    z, hi_lo = _sb_logits(q, kb, mask)
    return z, _sb_suffix_sums([hi_lo], tri2)[0]


def _sb_weighted(z, incl, carry, vb, mask):
    w = jnp.exp(z - (incl + carry))
    if mask is not None:
        w = jnp.where(mask, w, 0.0)
    return _dot(w.astype(BF16), vb)


def _sb_body(q_ref, k_ref, v_ref, tri_prev_ref, tri_half_ref, o_ref):
    qi = pl.program_id(1)
    hd, nh = HEAD_DIM_C, SB_HALF
    heads = range(SB_HEADS_PER_STEP)
    tri_prev = tri_prev_ref[...]
    tri_half = tri_half_ref[...]
    row_id = lax.broadcasted_iota(jnp.int32, (nh, nh), 0)
    col_id = lax.broadcasted_iota(jnp.int32, (nh, nh), 1)
    before = col_id < row_id
    cols = lambda h: slice(h * hd, (h + 1) * hd)
    rows = lambda half: slice(half * nh, (half + 1) * nh)

    def kv(start, size, h):
        start = pl.multiple_of(start, nh)
        return k_ref[pl.ds(start, size), cols(h)], v_ref[pl.ds(start, size), cols(h)]

    @pl.when(qi == 0)
    def _first_block():
        for h in heads:
            k0, v0 = kv(0, nh, h)
            k1, v1 = kv(nh, nh, h)
            z, incl = _sb_scores(q_ref[rows(0), cols(h)], k0, tri_half, before)
            o_ref[rows(0), cols(h)] = _sb_weighted(z, incl, 0.0, v0, before).astype(o_ref.dtype)
            q_bot = q_ref[rows(1), cols(h)]
            zd, incl_d = _sb_scores(q_bot, k1, tri_half, before)
            zp, incl_p = _sb_scores(q_bot, k0, tri_half, None)
            acc = (_sb_weighted(zd, incl_d, 0.0, v1, before)
                   + _sb_weighted(zp, incl_p, incl_d[:, 0:1], v0, None))
            o_ref[rows(1), cols(h)] = acc.astype(o_ref.dtype)

    @pl.when(qi > 0)
    def _later_blocks():
        base = qi * SB_TQ
        tiles = [(h, half) for h in heads for half in range(SB_TQ // nh)]
        window_start = lambda half: base - SB_PREV + half * nh
        qs = [q_ref[rows(half), cols(h)] for h, half in tiles]
        kv_prev = [kv(window_start(half), SB_PREV, h) for h, half in tiles]
        kv_diag = [kv(window_start(half) + SB_PREV, nh, h) for h, half in tiles]
        l_diag = [_sb_logits(qs[t], kv_diag[t][0], before) for t in range(len(tiles))]
        l_prev = [_sb_logits(qs[t], kv_prev[t][0], None) for t in range(len(tiles))]
        i_diag = _sb_suffix_sums([hl for _, hl in l_diag], tri_half)
        i_prev = _sb_suffix_sums([hl for _, hl in l_prev], tri_prev)
        state = ()
        for t in range(len(tiles)):
            zd, incl_d, zp, incl_p = l_diag[t][0], i_diag[t], l_prev[t][0], i_prev[t]
            carry = incl_d[:, 0:1]
            acc = (_sb_weighted(zd, incl_d, 0.0, kv_diag[t][1], before)
                   + _sb_weighted(zp, incl_p, carry, kv_prev[t][1], None))
            state += (carry + incl_p[:, 0:1], acc)

        def live(loop_state):
            n, state = loop_state
            stick = jnp.min(functools.reduce(jnp.minimum, state[0::2]))
            more_keys = window_start(1) - (n + 1) * nh >= 0
            return jnp.logical_and(more_keys, -stick > SB_LOG_F32_ZERO)

        def body(loop_state):
            n, state = loop_state
            starts = [window_start(half) - (n + 1) * nh for _, half in tiles]
            kvs = [kv(jnp.maximum(starts[t], 0), nh, h) for t, (h, _) in enumerate(tiles)]
            logits = [_sb_logits(qs[t], kvs[t][0], None) for t in range(len(tiles))]
            incls = _sb_suffix_sums([hl for _, hl in logits], tri_half)
            out = ()
            for t in range(len(tiles)):
                valid = starts[t] >= 0
                carry, acc = state[2 * t], state[2 * t + 1]
                add = _sb_weighted(logits[t][0], incls[t], carry, kvs[t][1], None)
                out += (carry + jnp.where(valid, incls[t][:, 0:1], 0.0),
                        acc + jnp.where(valid, add, 0.0))
            return n + 1, out

        _, state = lax.while_loop(live, body, (jnp.int32(0), state))
        for t, (h, half) in enumerate(tiles):
            o_ref[rows(half), cols(h)] = state[2 * t + 1].astype(o_ref.dtype)


def _suffix_sum_matrix(n):
    jj = jnp.arange(n)
    tri = (jj[:, None] >= jj[None, :]).astype(BF16)
    return jnp.concatenate([tri, tri], axis=0)


def _stick_breaking(proj1):
    s = proj1.shape[0]
    hw = SB_HEADS_PER_STEP * HEAD_DIM_C
    n_groups = N_HEADS_C // SB_HEADS_PER_STEP
    whole = lambda shape: pl.BlockSpec(shape, lambda g, i: (0,) * len(shape))
    return pl.pallas_call(
        _sb_body,
        grid=(n_groups, s // SB_TQ),
        in_specs=[pl.BlockSpec((SB_TQ, hw), lambda g, i: (i, g)),
                  pl.BlockSpec((s, hw), lambda g, i: (0, n_groups + g)),
                  pl.BlockSpec((s, hw), lambda g, i: (0, 2 * n_groups + g)),
                  whole((2 * SB_PREV, SB_PREV)), whole((2 * SB_HALF, SB_HALF))],
        out_specs=pl.BlockSpec((SB_TQ, hw), lambda g, i: (i, g)),
        out_shape=jax.ShapeDtypeStruct((s, N_HEADS_C * HEAD_DIM_C), BF16),
        compiler_params=_params(2),
        name="stick_breaking",
    )(proj1, proj1, proj1, _suffix_sum_matrix(SB_PREV), _suffix_sum_matrix(SB_HALF))


def kernel(x, rel_bias, norm_mix, even_w_in, even_conv_w, even_conv_b, even_gate_a_w,
           even_gate_a_b, even_gate_x_w, even_gate_x_b, even_lru_lambda, even_sinks,
           even_w_out, odd_w_in, odd_w_out, norm_mlp, w_up, w_down, final_norm):
    b, s, d = x.shape
    h = x.reshape(b * s, d)

    proj0 = _norm_matmul(h, norm_mix[0], even_w_in[0], F32)
    y_a = _swa(proj0, rel_bias, even_sinks[0], _swa_bucket_table())
    y_b = _lru(proj0, even_conv_w[0], even_conv_b[0],
               _pair_block_diag(even_gate_a_w[0]).astype(BF16), even_gate_a_b[0],
               _pair_block_diag(even_gate_x_w[0]).astype(BF16), even_gate_x_b[0],
               even_lru_lambda[0])
    h = _matmul_res([y_a, y_b], even_w_out[0], h)
    h = _mlp(h, norm_mlp[0], w_up, w_down, 0, None)

    proj1 = _norm_matmul(h, norm_mix[1], odd_w_in[0], BF16)
    y = _stick_breaking(proj1)
    h = _matmul_res([y], odd_w_out[0], h)
    h = _mlp(h, norm_mlp[1], w_up, w_down, 1, final_norm)
    return h.reshape(b, s, d)
```

```python
import functools
import math

import jax
import jax.numpy as jnp
from jax import lax
from jax.experimental import pallas as pl
from jax.experimental.pallas import tpu as pltpu

F32 = jnp.float32
BF16 = jnp.bfloat16

D_MODEL = 2048
CHUNK = 64
EPS = 1e-6

HEAD_DIM_A = 64
N_HEADS_A = 16
N_KV_A = 4
GROUP_A = N_HEADS_A // N_KV_A
WINDOW_CHUNKS = 2
Q_BLOCK = 128
HIST = WINDOW_CHUNKS * CHUNK
KEY_SPAN = Q_BLOCK + HIST
NUM_BUCKETS = 32
MAX_DISTANCE = 128

LRU_WIDTH = 1024
LRU_BLOCKS = 16
LRU_BLOCK_DIM = LRU_WIDTH // LRU_BLOCKS
CONV_WIDTH = 4
LRU_C = 8.0

HEAD_DIM_C = 128
N_HEADS_C = 16

Q_COLS = N_HEADS_A * HEAD_DIM_A
KV_COLS = N_KV_A * HEAD_DIM_A
OFF_K = Q_COLS
OFF_V = OFF_K + KV_COLS
OFF_X = OFF_V + KV_COLS
OFF_G = OFF_X + LRU_WIDTH

V7X_LANES = 128
V7X_SUBLANES = 8
V7X_VMEM_LIMIT_BYTES = 56 * 1024 * 1024

MASK_NEG = -1e30
LOG2_E = math.log2(math.e)

PROJ_TM = 512
PROJ_TN_MAX = 1024
PROJ_LOAD_COLS = 256
MLP_TM, MLP_TF = 1024, 512
LRU_TIME_BLOCK = 512
LRU_COL_BLOCK = 512
SB_TQ = 256
SB_HALF = 128
SB_PREV = 256
SB_HEADS_PER_STEP = 4


def _params(n_grid_dims):
    return pltpu.CompilerParams(
        dimension_semantics=("arbitrary",) * n_grid_dims,
        vmem_limit_bytes=V7X_VMEM_LIMIT_BYTES)


def _proj_tn(n):
    tn = PROJ_TN_MAX
    while n % tn:
        tn //= 2
    return tn


def _rms_norm_f32(x, g):
    ms = jnp.mean(x * x, axis=-1, keepdims=True)
    return (x * lax.rsqrt(ms + EPS)) * g


def _dot(a, b):
    return jnp.dot(a, b, preferred_element_type=F32)


def _dot_nt(a, b):
    return lax.dot_general(a, b, (((1,), (1,)), ((), ())), preferred_element_type=F32)


def _resident(shape):
    return pl.BlockSpec(shape, lambda i: (0,) * len(shape), pipeline_mode=pl.Buffered(1))


def _weight_scratch(k, n):
    return [pltpu.VMEM((k, n), BF16),
            pltpu.VMEM((2, k, PROJ_LOAD_COLS), F32),
            pltpu.SemaphoreType.DMA((2,))]


def _for_each_column_chunk(w_hbm, w_ref, stage_ref, sem, tn, compute):
    n = w_ref.shape[1]
    lc = PROJ_LOAD_COLS
    n_loads, loads_per_chunk = n // lc, tn // lc

    def copy(c):
        return pltpu.make_async_copy(w_hbm.at[:, pl.ds(c * lc, lc)], stage_ref.at[c % 2],
                                     sem.at[c % 2])

    @pl.when(pl.program_id(0) == 0)
    def _():
        copy(0).start()
        for c in range(n // tn):
            for l in range(c * loads_per_chunk, (c + 1) * loads_per_chunk):
                if l + 1 < n_loads:
                    copy(l + 1).start()
                copy(l).wait()
                w_ref[:, l * lc:(l + 1) * lc] = stage_ref[l % 2].astype(BF16)
            compute(slice(c * tn, (c + 1) * tn))

    @pl.when(pl.program_id(0) > 0)
    def _():
        for c in range(n // tn):
            compute(slice(c * tn, (c + 1) * tn))


def _norm_matmul_body(x_ref, g_ref, w_hbm, o_ref, hn_ref, w_ref, stage_ref, sem):
    hn_ref[...] = _rms_norm_f32(x_ref[...], g_ref[...]).astype(BF16)

    def compute(cs):
        o_ref[:, cs] = _dot(hn_ref[...], w_ref[:, cs]).astype(o_ref.dtype)

    _for_each_column_chunk(w_hbm, w_ref, stage_ref, sem, _proj_tn(o_ref.shape[1]), compute)


def _norm_matmul(x, g, w, out_dtype):
    m, k = x.shape
    n = w.shape[1]
    tm = PROJ_TM
    return pl.pallas_call(
        _norm_matmul_body,
        grid=(m // tm,),
        in_specs=[pl.BlockSpec((tm, k), lambda i: (i, 0)),
                  _resident((1, k)), pl.BlockSpec(memory_space=pl.ANY)],
        out_specs=pl.BlockSpec((tm, n), lambda i: (i, 0)),
        out_shape=jax.ShapeDtypeStruct((m, n), out_dtype),
        scratch_shapes=[pltpu.VMEM((tm, k), BF16)] + _weight_scratch(k, n),
        compiler_params=_params(1),
        name="norm_matmul",
    )(x, g.reshape(1, k), w)


def _matmul_res_body(*refs, n_parts):
    a_refs = refs[:n_parts]
    w_hbm, res_ref, o_ref, w_ref, stage_ref, sem = refs[n_parts:]
    kp = w_ref.shape[0] // n_parts

    def compute(cs):
        acc = res_ref[:, cs]
        for p, a_ref in enumerate(a_refs):
            acc = acc + _dot(a_ref[...], w_ref[p * kp:(p + 1) * kp, cs])
        o_ref[:, cs] = acc

    _for_each_column_chunk(w_hbm, w_ref, stage_ref, sem, _proj_tn(o_ref.shape[1]), compute)


def _matmul_res(a_list, w, res):
    m, n = res.shape
    tm = PROJ_TM
    n_parts = len(a_list)
    kp = w.shape[0] // n_parts
    in_specs = [pl.BlockSpec((tm, kp), lambda i: (i, 0)) for _ in a_list]
    in_specs += [pl.BlockSpec(memory_space=pl.ANY), pl.BlockSpec((tm, n), lambda i: (i, 0))]
    return pl.pallas_call(
        functools.partial(_matmul_res_body, n_parts=n_parts),
        grid=(m // tm,),
        in_specs=in_specs,
        out_specs=pl.BlockSpec((tm, n), lambda i: (i, 0)),
        out_shape=jax.ShapeDtypeStruct((m, n), F32),
        scratch_shapes=_weight_scratch(w.shape[0], n),
        compiler_params=_params(1),
        name="matmul_res",
    )(*a_list, w, res)


def _mlp_body(*refs, final_norm):
    if final_norm:
        h_ref, g_ref, wu_ref, wd_ref, fg_ref, o_ref, hn_ref = refs
    else:
        h_ref, g_ref, wu_ref, wd_ref, o_ref, hn_ref = refs
    f = pl.program_id(1)

    @pl.when(f == 0)
    def _():
        h = h_ref[...]
        hn_ref[...] = _rms_norm_f32(h, g_ref[...]).astype(BF16)
        o_ref[...] = h

    up = _dot(hn_ref[...], wu_ref[...].astype(BF16))
    act = jnp.square(jnp.maximum(up, 0.0)).astype(BF16)
    o_ref[...] += _dot(act, wd_ref[...].astype(BF16))

    if final_norm:
        @pl.when(f == pl.num_programs(1) - 1)
        def _():
            o_ref[...] = _rms_norm_f32(o_ref[...], fg_ref[...])


def _mlp(h, g, w_up, w_down, layer, final_g):
    m, d = h.shape
    ff = w_up.shape[2]
    tm, tf = MLP_TM, MLP_TF
    final_norm = final_g is not None
    in_specs = [pl.BlockSpec((tm, d), lambda i, f: (i, 0)),
                pl.BlockSpec((1, d), lambda i, f: (0, 0)),
                pl.BlockSpec((None, d, tf), lambda i, f: (layer, 0, f)),
                pl.BlockSpec((None, tf, d), lambda i, f: (layer, f, 0))]
    args = [h, g.reshape(1, d), w_up, w_down]
    if final_norm:
        in_specs.append(pl.BlockSpec((1, d), lambda i, f: (0, 0)))
        args.append(final_g.reshape(1, d))
    return pl.pallas_call(
        functools.partial(_mlp_body, final_norm=final_norm),
        grid=(m // tm, ff // tf),
        in_specs=in_specs,
        out_specs=pl.BlockSpec((tm, d), lambda i, f: (i, 0)),
        out_shape=jax.ShapeDtypeStruct((m, d), F32),
        scratch_shapes=[pltpu.VMEM((tm, d), BF16)],
        compiler_params=_params(2),
        name="mlp",
    )(*args)


def _swa_body(relb_ref, sink_ref, bucket_ref, q_ref, kc_ref, kp_ref, vc_ref, vp_ref,
              o_ref, tab_ref):
    i = pl.program_id(0)
    hd, nq = HEAD_DIM_A, Q_BLOCK
    lanes = GROUP_A * nq

    @pl.when(i == 0)
    def _():
        bucket = bucket_ref[...]
        for head in range(N_HEADS_A):
            t = jnp.full((KEY_SPAN, nq), MASK_NEG, F32)
            for b in range(NUM_BUCKETS):
                t = jnp.where(bucket == b, relb_ref[b, head], t)
            tab_ref[:, head * nq:(head + 1) * nq] = t

    hist_pen = jnp.where(i == 0, MASK_NEG, 0.0).astype(F32)

    q_t = (q_ref[...] * (hd ** -0.5)).T
    k_all = jnp.concatenate([kp_ref[...], kc_ref[...]], axis=0).astype(BF16)
    v_t = jnp.concatenate([vp_ref[...].T, vc_ref[...].T], axis=1)
    zeros = jnp.zeros((hd, lanes), BF16)
    ones = jnp.ones((2 * V7X_SUBLANES, KEY_SPAN), BF16)

    kv_heads = range(N_KV_A)
    heads = [[h * GROUP_A + g for g in range(GROUP_A)] for h in kv_heads]
    rhs_rows = []
    for h in kv_heads:
        qg = jnp.concatenate([q_t[hd_ * hd:(hd_ + 1) * hd, :] for hd_ in heads[h]],
                             axis=1).astype(BF16)
        rhs_rows.append(jnp.concatenate([zeros] * h + [qg] + [zeros] * (N_KV_A - 1 - h), axis=1))
    logits = _dot(k_all, jnp.concatenate(rhs_rows, axis=0)) + tab_ref[...]
    logits = jnp.concatenate([logits[0:HIST] + hist_pen, logits[HIST:KEY_SPAN]], axis=0)
    sink = jnp.concatenate([jnp.full((1, nq), sink_ref[hd_], F32)
                            for h in kv_heads for hd_ in heads[h]], axis=1)
    m = jnp.maximum(jnp.max(logits, axis=0, keepdims=True), sink)
    p = jnp.exp(logits - m).astype(BF16)
    sink_p = jnp.exp(sink - m)
    for h in kv_heads:
        hs = slice(h * lanes, (h + 1) * lanes)
        v_aug = jnp.concatenate([v_t[h * hd:(h + 1) * hd, :].astype(BF16), ones], axis=0)
        acc = _dot(v_aug, p[:, hs])
        out_t = acc[0:hd, :] * (1.0 / (acc[hd:hd + 1, :] + sink_p[:, hs]))
        for g in range(0, GROUP_A, 2):
            both = jnp.concatenate([out_t[:, g * nq:(g + 1) * nq],
                                    out_t[:, (g + 1) * nq:(g + 2) * nq]], axis=0)
            c0 = heads[h][g] * hd
            o_ref[:, c0:c0 + 2 * hd] = both.T.astype(o_ref.dtype)


def _swa(proj0, rel_bias, sinks, bucket):
    s = proj0.shape[0]
    nb = s // Q_BLOCK
    k_blk, v_blk = OFF_K // KV_COLS, OFF_V // KV_COLS
    kv_spec = lambda blk, prev: pl.BlockSpec(
        (Q_BLOCK, KV_COLS),
        (lambda i: (jnp.maximum(i - 1, 0), blk)) if prev else (lambda i: (i, blk)))
    return pl.pallas_call(
        _swa_body,
        grid=(nb,),
        in_specs=[pl.BlockSpec(memory_space=pltpu.SMEM),
                  pl.BlockSpec(memory_space=pltpu.SMEM),
                  pl.BlockSpec((KEY_SPAN, Q_BLOCK), lambda i: (0, 0)),
                  pl.BlockSpec((Q_BLOCK, Q_COLS), lambda i: (i, 0)),
                  kv_spec(k_blk, False), kv_spec(k_blk, True),
                  kv_spec(v_blk, False), kv_spec(v_blk, True)],
        out_specs=pl.BlockSpec((Q_BLOCK, Q_COLS), lambda i: (i, 0)),
        out_shape=jax.ShapeDtypeStruct((s, Q_COLS), BF16),
        scratch_shapes=[pltpu.VMEM((KEY_SPAN, N_HEADS_A * Q_BLOCK), F32)],
        compiler_params=_params(1),
        name="swa_attention",
    )(rel_bias, sinks, bucket, proj0, proj0, proj0, proj0, proj0)


def _t5_bucket(rel):
    half = NUM_BUCKETS // 2
    ret = jnp.where(rel > 0, half, 0)
    n = jnp.abs(rel)
    max_exact = half // 2
    nf = jnp.maximum(n, 1).astype(F32)
    large = max_exact + (jnp.log(nf / max_exact) / math.log(MAX_DISTANCE / max_exact)
                         * (half - max_exact)).astype(jnp.int32)
    large = jnp.minimum(large, half - 1)
    return ret + jnp.where(n < max_exact, n, large)


def _swa_bucket_table():
    qi = jnp.arange(Q_BLOCK)
    sj = jnp.arange(KEY_SPAN)
    rel = (sj[:, None] - HIST) - qi[None, :]
    q_chunk = qi // CHUNK
    k_chunk = (sj - HIST) // CHUNK
    in_window = ((k_chunk[:, None] <= q_chunk[None, :])
                 & (k_chunk[:, None] >= q_chunk[None, :] - WINDOW_CHUNKS))
    return jnp.where(in_window, _t5_bucket(rel), -1).astype(jnp.int32)


LRU_PAD = V7X_SUBLANES
LRU_LANE_TILES = LRU_WIDTH // V7X_LANES
LRU_COL_BLOCKS = LRU_WIDTH // LRU_COL_BLOCK


def _neg_expm1(y, e):
    safe_log = jnp.where(e == 1.0, 1.0, jnp.log(e))
    near_zero = jnp.where(e == 1.0, -y, (1.0 - e) * y / safe_log)
    return jnp.where(y > -0.5, near_zero, 1.0 - e)


def _lru_body(*refs):
    xb_refs = refs[:LRU_COL_BLOCKS]
    gb_refs = refs[LRU_COL_BLOCKS:2 * LRU_COL_BLOCKS]
    (cw_ref, cb_ref, wa_ref, ba_ref, wx_ref, bx_ref, lam_ref,
     o_ref, xpad_ref, a_ref, u_ref, hs_ref, hc_ref) = refs[2 * LRU_COL_BLOCKS:]
    t = pl.program_id(0)
    tb, cb = LRU_TIME_BLOCK, LRU_WIDTH

    @pl.when(t == 0)
    def _():
        xpad_ref[0:LRU_PAD, :] = jnp.zeros((LRU_PAD, cb), F32)
        hc_ref[...] = jnp.zeros((1, cb), F32)

    x = jnp.concatenate([r[...] for r in xb_refs], axis=1)
    gb = jnp.concatenate([r[...] for r in gb_refs], axis=1)
    xpad_ref[LRU_PAD:LRU_PAD + tb, :] = x
    xc = cb_ref[...] + cw_ref[CONV_WIDTH - 1:CONV_WIDTH, :] * x
    for w in range(CONV_WIDTH - 1):
        off = LRU_PAD - (CONV_WIDTH - 1) + w
        xc = xc + cw_ref[w:w + 1, :] * xpad_ref[off:off + tb, :]
    xpad_ref[0:LRU_PAD, :] = x[tb - LRU_PAD:tb, :]

    neg_c_softplus = -LRU_C * jax.nn.softplus(-lam_ref[...])
    for c in range(LRU_LANE_TILES):
        cs = slice(c * V7X_LANES, (c + 1) * V7X_LANES)
        xcc = xc[:, cs]
        xcb = xcc.astype(BF16)
        r = jax.nn.sigmoid(_dot(xcb, wa_ref[c]) + ba_ref[:, cs])
        ig = jax.nn.sigmoid(_dot(xcb, wx_ref[c]) + bx_ref[:, cs])
        log_a = r * neg_c_softplus[:, cs]
        a = jnp.exp(log_a)
        a_ref[:, cs] = a
        u_ref[:, cs] = jnp.sqrt(_neg_expm1(2.0 * log_a, a * a)) * (ig * xcc)

    def scan_rows(r8, h):
        base = pl.multiple_of(r8 * V7X_SUBLANES, V7X_SUBLANES)
        for r in range(V7X_SUBLANES):
            h = a_ref[pl.ds(base + r, 1), :] * h + u_ref[pl.ds(base + r, 1), :]
            hs_ref[pl.ds(base + r, 1), :] = h
        return h

    hc_ref[...] = lax.fori_loop(0, tb // V7X_SUBLANES, scan_rows, hc_ref[...])
    o_ref[...] = (jax.nn.gelu(gb, approximate=True) * hs_ref[...]).astype(o_ref.dtype)


def _lru(proj0, conv_w, conv_b, wa_pairs, ba, wx_pairs, bx, lam):
    s = proj0.shape[0]
    tb, cb, wb = LRU_TIME_BLOCK, LRU_WIDTH, LRU_COL_BLOCK
    row = lambda v: v.reshape(1, cb)
    whole = lambda shape: pl.BlockSpec(shape, lambda t: (0,) * len(shape))
    col_blocks = lambda off: [
        pl.BlockSpec((tb, wb), functools.partial(lambda t, c: (t, c), c=off // wb + c))
        for c in range(LRU_COL_BLOCKS)]
    return pl.pallas_call(
        _lru_body,
        grid=(s // tb,),
        in_specs=col_blocks(OFF_X) + col_blocks(OFF_G) + [
            whole((CONV_WIDTH, cb)), whole((1, cb)),
            whole(wa_pairs.shape), whole((1, cb)),
            whole(wx_pairs.shape), whole((1, cb)), whole((1, cb))],
        out_specs=pl.BlockSpec((tb, cb), lambda t: (t, 0)),
        out_shape=jax.ShapeDtypeStruct((s, cb), BF16),
        scratch_shapes=[pltpu.VMEM((tb + LRU_PAD, cb), F32),
                        pltpu.VMEM((tb, cb), F32), pltpu.VMEM((tb, cb), F32),
                        pltpu.VMEM((tb, cb), F32), pltpu.VMEM((1, cb), F32)],
        compiler_params=_params(1),
        name="rg_lru",
    )(*([proj0] * (2 * LRU_COL_BLOCKS)), conv_w, row(conv_b), wa_pairs, row(ba),
      wx_pairs, row(bx), row(lam))


def _pair_block_diag(w):
    nb, bd, _ = w.shape
    wp = w.reshape(nb // 2, 2, bd, bd)
    z = jnp.zeros((nb // 2, bd, bd), w.dtype)
    top = jnp.concatenate([wp[:, 0], z], axis=2)
    bot = jnp.concatenate([z, wp[:, 1]], axis=2)
    return jnp.concatenate([top, bot], axis=1)


SB_LOG_F32_ZERO = -105.0


def _sb_logits(q, kb, mask):
    z = _dot_nt(q, kb) * (HEAD_DIM_C ** -0.5)
    sp = jnp.maximum(z, 0.0) + jnp.log(1.0 + jnp.exp2(jnp.abs(z) * -LOG2_E))
    if mask is not None:
        sp = jnp.where(mask, sp, 0.0)
    hi = sp.astype(BF16)
    lo = (sp - hi.astype(F32)).astype(BF16)
    return z, jnp.concatenate([hi, lo], axis=1)


def _sb_suffix_sums(hi_lo_list, tri2):
    rows = hi_lo_list[0].shape[0]
    out = _dot(jnp.concatenate(hi_lo_list, axis=0), tri2)
    return [out[t * rows:(t + 1) * rows] for t in range(len(hi_lo_list))]


def _sb_scores(q, kb, tri2, mask):
    z, hi_lo = _sb_logits(q, kb, mask)
    return z, _sb_suffix_sums([hi_lo], tri2)[0]


def _sb_weighted(z, incl, carry, vb, mask):
    w = jnp.exp(z - (incl + carry))
    if mask is not None:
        w = jnp.where(mask, w, 0.0)
    return _dot(w.astype(BF16), vb)


def _sb_body(q_ref, k_ref, v_ref, tri_prev_ref, tri_half_ref, o_ref):
    qi = pl.program_id(1)
    hd, nh = HEAD_DIM_C, SB_HALF
    heads = range(SB_HEADS_PER_STEP)
    tri_prev = tri_prev_ref[...]
    tri_half = tri_half_ref[...]
    row_id = lax.broadcasted_iota(jnp.int32, (nh, nh), 0)
    col_id = lax.broadcasted_iota(jnp.int32, (nh, nh), 1)
    before = col_id < row_id
    cols = lambda h: slice(h * hd, (h + 1) * hd)
    rows = lambda half: slice(half * nh, (half + 1) * nh)

    def kv(start, size, h):
        start = pl.multiple_of(start, nh)
        return k_ref[pl.ds(start, size), cols(h)], v_ref[pl.ds(start, size), cols(h)]

    @pl.when(qi == 0)
    def _first_block():
        for h in heads:
            k0, v0 = kv(0, nh, h)
            k1, v1 = kv(nh, nh, h)
            z, incl = _sb_scores(q_ref[rows(0), cols(h)], k0, tri_half, before)
            o_ref[rows(0), cols(h)] = _sb_weighted(z, incl, 0.0, v0, before).astype(o_ref.dtype)
            q_bot = q_ref[rows(1), cols(h)]
            zd, incl_d = _sb_scores(q_bot, k1, tri_half, before)
            zp, incl_p = _sb_scores(q_bot, k0, tri_half, None)
            acc = (_sb_weighted(zd, incl_d, 0.0, v1, before)
                   + _sb_weighted(zp, incl_p, incl_d[:, 0:1], v0, None))
            o_ref[rows(1), cols(h)] = acc.astype(o_ref.dtype)

    @pl.when(qi > 0)
    def _later_blocks():
        base = qi * SB_TQ
        tiles = [(h, half) for h in heads for half in range(SB_TQ // nh)]
        window_start = lambda half: base - SB_PREV + half * nh
        qs = [q_ref[rows(half), cols(h)] for h, half in tiles]
        kv_prev = [kv(window_start(half), SB_PREV, h) for h, half in tiles]
        kv_diag = [kv(window_start(half) + SB_PREV, nh, h) for h, half in tiles]
        l_diag = [_sb_logits(qs[t], kv_diag[t][0], before) for t in range(len(tiles))]
        l_prev = [_sb_logits(qs[t], kv_prev[t][0], None) for t in range(len(tiles))]
        i_diag = _sb_suffix_sums([hl for _, hl in l_diag], tri_half)
        i_prev = _sb_suffix_sums([hl for _, hl in l_prev], tri_prev)
        state = ()
        for t in range(len(tiles)):
            zd, incl_d, zp, incl_p = l_diag[t][0], i_diag[t], l_prev[t][0], i_prev[t]
            carry = incl_d[:, 0:1]
            acc = (_sb_weighted(zd, incl_d, 0.0, kv_diag[t][1], before)
                   + _sb_weighted(zp, incl_p, carry, kv_prev[t][1], None))
            state += (carry + incl_p[:, 0:1], acc)

        def live(loop_state):
            n, state = loop_state
            stick = jnp.min(functools.reduce(jnp.minimum, state[0::2]))
            more_keys = window_start(1) - (n + 1) * nh >= 0
            return jnp.logical_and(more_keys, -stick > SB_LOG_F32_ZERO)

        def body(loop_state):
            n, state = loop_state
            starts = [window_start(half) - (n + 1) * nh for _, half in tiles]
            kvs = [kv(jnp.maximum(starts[t], 0), nh, h) for t, (h, _) in enumerate(tiles)]
            logits = [_sb_logits(qs[t], kvs[t][0], None) for t in range(len(tiles))]
            incls = _sb_suffix_sums([hl for _, hl in logits], tri_half)
            out = ()
            for t in range(len(tiles)):
                valid = starts[t] >= 0
                carry, acc = state[2 * t], state[2 * t + 1]
                add = _sb_weighted(logits[t][0], incls[t], carry, kvs[t][1], None)
                out += (carry + jnp.where(valid, incls[t][:, 0:1], 0.0),
                        acc + jnp.where(valid, add, 0.0))
            return n + 1, out

        _, state = lax.while_loop(live, body, (jnp.int32(0), state))
        for t, (h, half) in enumerate(tiles):
            o_ref[rows(half), cols(h)] = state[2 * t + 1].astype(o_ref.dtype)


def _suffix_sum_matrix(n):
    jj = jnp.arange(n)
    tri = (jj[:, None] >= jj[None, :]).astype(BF16)
    return jnp.concatenate([tri, tri], axis=0)


def _stick_breaking(proj1):
    s = proj1.shape[0]
    hw = SB_HEADS_PER_STEP * HEAD_DIM_C
    n_groups = N_HEADS_C // SB_HEADS_PER_STEP
    whole = lambda shape: pl.BlockSpec(shape, lambda g, i: (0,) * len(shape))
    return pl.pallas_call(
        _sb_body,
        grid=(n_groups, s // SB_TQ),
        in_specs=[pl.BlockSpec((SB_TQ, hw), lambda g, i: (i, g)),
                  pl.BlockSpec((s, hw), lambda g, i: (0, n_groups + g)),
                  pl.BlockSpec((s, hw), lambda g, i: (0, 2 * n_groups + g)),
                  whole((2 * SB_PREV, SB_PREV)), whole((2 * SB_HALF, SB_HALF))],
        out_specs=pl.BlockSpec((SB_TQ, hw), lambda g, i: (i, g)),
        out_shape=jax.ShapeDtypeStruct((s, N_HEADS_C * HEAD_DIM_C), BF16),
        compiler_params=_params(2),
        name="stick_breaking",
    )(proj1, proj1, proj1, _suffix_sum_matrix(SB_PREV), _suffix_sum_matrix(SB_HALF))


def kernel(x, rel_bias, norm_mix, even_w_in, even_conv_w, even_conv_b, even_gate_a_w,
           even_gate_a_b, even_gate_x_w, even_gate_x_b, even_lru_lambda, even_sinks,
           even_w_out, odd_w_in, odd_w_out, norm_mlp, w_up, w_down, final_norm):
    b, s, d = x.shape
    h = x.reshape(b * s, d)

    proj0 = _norm_matmul(h, norm_mix[0], even_w_in[0], F32)
    y_a = _swa(proj0, rel_bias, even_sinks[0], _swa_bucket_table())
    y_b = _lru(proj0, even_conv_w[0], even_conv_b[0],
               _pair_block_diag(even_gate_a_w[0]).astype(BF16), even_gate_a_b[0],
               _pair_block_diag(even_gate_x_w[0]).astype(BF16), even_gate_x_b[0],
               even_lru_lambda[0])
    h = _matmul_res([y_a, y_b], even_w_out[0], h)
    h = _mlp(h, norm_mlp[0], w_up, w_down, 0, None)

    proj1 = _norm_matmul(h, norm_mix[1], odd_w_in[0], BF16)
    y = _stick_breaking(proj1)
    h = _matmul_res([y], odd_w_out[0], h)
    h = _mlp(h, norm_mlp[1], w_up, w_down, 1, final_norm)
    return h.reshape(b, s, d)
```

```python
import functools
import math

import jax
import jax.numpy as jnp
from jax import lax
from jax.experimental import pallas as pl
from jax.experimental.pallas import tpu as pltpu

F32 = jnp.float32
BF16 = jnp.bfloat16

D_MODEL = 2048
CHUNK = 64
EPS = 1e-6

HEAD_DIM_A = 64
N_HEADS_A = 16
N_KV_A = 4
GROUP_A = N_HEADS_A // N_KV_A
WINDOW_CHUNKS = 2
Q_BLOCK = 128
HIST = WINDOW_CHUNKS * CHUNK
KEY_SPAN = Q_BLOCK + HIST
NUM_BUCKETS = 32
MAX_DISTANCE = 128

LRU_WIDTH = 1024
LRU_BLOCKS = 16
LRU_BLOCK_DIM = LRU_WIDTH // LRU_BLOCKS
CONV_WIDTH = 4
LRU_C = 8.0

HEAD_DIM_C = 128
N_HEADS_C = 16

Q_COLS = N_HEADS_A * HEAD_DIM_A
KV_COLS = N_KV_A * HEAD_DIM_A
OFF_K = Q_COLS
OFF_V = OFF_K + KV_COLS
OFF_X = OFF_V + KV_COLS
OFF_G = OFF_X + LRU_WIDTH

V7X_LANES = 128
V7X_SUBLANES = 8
V7X_VMEM_LIMIT_BYTES = 56 * 1024 * 1024

MASK_NEG = -1e30
LOG2_E = math.log2(math.e)

PROJ_TM = 512
PROJ_TN_MAX = 1024
PROJ_LOAD_COLS = 256
PROJ_LOAD_SLOTS = 3
MLP_TM, MLP_TF = 1024, 512
LRU_TIME_BLOCK = 512
LRU_COL_BLOCK = 512
SB_TQ = 256
SB_HALF = 128
SB_PREV = 256
SB_HEADS_PER_STEP = 4


def _params(n_grid_dims):
    return pltpu.CompilerParams(
        dimension_semantics=("arbitrary",) * n_grid_dims,
        vmem_limit_bytes=V7X_VMEM_LIMIT_BYTES)


def _proj_tn(n):
    tn = PROJ_TN_MAX
    while n % tn:
        tn //= 2
    return tn


def _rms_norm_f32(x, g):
    ms = jnp.mean(x * x, axis=-1, keepdims=True)
    return (x * lax.rsqrt(ms + EPS)) * g


def _dot(a, b):
    return jnp.dot(a, b, preferred_element_type=F32)


def _dot_nt(a, b):
    return lax.dot_general(a, b, (((1,), (1,)), ((), ())), preferred_element_type=F32)


def _resident(shape):
    return pl.BlockSpec(shape, lambda i: (0,) * len(shape), pipeline_mode=pl.Buffered(1))


def _weight_scratch(k, n):
    return [pltpu.VMEM((k, n), BF16),
            pltpu.VMEM((PROJ_LOAD_SLOTS, k, PROJ_LOAD_COLS), F32),
            pltpu.SemaphoreType.DMA((PROJ_LOAD_SLOTS,))]


def _for_each_column_chunk(w_hbm, w_ref, stage_ref, sem, tn, compute):
    n = w_ref.shape[1]
    lc, slots = PROJ_LOAD_COLS, PROJ_LOAD_SLOTS
    n_loads, loads_per_chunk = n // lc, tn // lc

    def copy(c):
        return pltpu.make_async_copy(w_hbm.at[:, pl.ds(c * lc, lc)], stage_ref.at[c % slots],
                                     sem.at[c % slots])

    @pl.when(pl.program_id(0) == 0)
    def _():
        for l in range(min(slots - 1, n_loads)):
            copy(l).start()
        for c in range(n // tn):
            for l in range(c * loads_per_chunk, (c + 1) * loads_per_chunk):
                if l + slots - 1 < n_loads:
                    copy(l + slots - 1).start()
                copy(l).wait()
                w_ref[:, l * lc:(l + 1) * lc] = stage_ref[l % slots].astype(BF16)
            compute(slice(c * tn, (c + 1) * tn))

    @pl.when(pl.program_id(0) > 0)
    def _():
        for c in range(n // tn):
            compute(slice(c * tn, (c + 1) * tn))


def _norm_matmul_body(x_ref, g_ref, w_hbm, o_ref, hn_ref, w_ref, stage_ref, sem):
    hn_ref[...] = _rms_norm_f32(x_ref[...], g_ref[...]).astype(BF16)

    def compute(cs):
        o_ref[:, cs] = _dot(hn_ref[...], w_ref[:, cs]).astype(o_ref.dtype)

    _for_each_column_chunk(w_hbm, w_ref, stage_ref, sem, _proj_tn(o_ref.shape[1]), compute)


def _norm_matmul(x, g, w, out_dtype):
    m, k = x.shape
    n = w.shape[1]
    tm = PROJ_TM
    return pl.pallas_call(
        _norm_matmul_body,
        grid=(m // tm,),
        in_specs=[pl.BlockSpec((tm, k), lambda i: (i, 0)),
                  _resident((1, k)), pl.BlockSpec(memory_space=pl.ANY)],
        out_specs=pl.BlockSpec((tm, n), lambda i: (i, 0)),
        out_shape=jax.ShapeDtypeStruct((m, n), out_dtype),
        scratch_shapes=[pltpu.VMEM((tm, k), BF16)] + _weight_scratch(k, n),
        compiler_params=_params(1),
        name="norm_matmul",
    )(x, g.reshape(1, k), w)


def _matmul_res_body(*refs, n_parts):
    a_refs = refs[:n_parts]
    w_hbm, res_ref, o_ref, w_ref, stage_ref, sem = refs[n_parts:]
    kp = w_ref.shape[0] // n_parts

    def compute(cs):
        acc = res_ref[:, cs]
        for p, a_ref in enumerate(a_refs):
            acc = acc + _dot(a_ref[...], w_ref[p * kp:(p + 1) * kp, cs])
        o_ref[:, cs] = acc

    _for_each_column_chunk(w_hbm, w_ref, stage_ref, sem, _proj_tn(o_ref.shape[1]), compute)


def _matmul_res(a_list, w, res):
    m, n = res.shape
    tm = PROJ_TM
    n_parts = len(a_list)
    kp = w.shape[0] // n_parts
    in_specs = [pl.BlockSpec((tm, kp), lambda i: (i, 0)) for _ in a_list]
    in_specs += [pl.BlockSpec(memory_space=pl.ANY), pl.BlockSpec((tm, n), lambda i: (i, 0))]
    return pl.pallas_call(
        functools.partial(_matmul_res_body, n_parts=n_parts),
        grid=(m // tm,),
        in_specs=in_specs,
        out_specs=pl.BlockSpec((tm, n), lambda i: (i, 0)),
        out_shape=jax.ShapeDtypeStruct((m, n), F32),
        scratch_shapes=_weight_scratch(w.shape[0], n),
        compiler_params=_params(1),
        name="matmul_res",
    )(*a_list, w, res)


def _mlp_body(*refs, final_norm):
    if final_norm:
        h_ref, g_ref, wu_ref, wd_ref, fg_ref, o_ref, hn_ref = refs
    else:
        h_ref, g_ref, wu_ref, wd_ref, o_ref, hn_ref = refs
    f = pl.program_id(1)

    @pl.when(f == 0)
    def _():
        h = h_ref[...]
        hn_ref[...] = _rms_norm_f32(h, g_ref[...]).astype(BF16)
        o_ref[...] = h

    up = _dot(hn_ref[...], wu_ref[...].astype(BF16))
    act = jnp.square(jnp.maximum(up, 0.0)).astype(BF16)
    o_ref[...] += _dot(act, wd_ref[...].astype(BF16))

    if final_norm:
        @pl.when(f == pl.num_programs(1) - 1)
        def _():
            o_ref[...] = _rms_norm_f32(o_ref[...], fg_ref[...])


def _mlp(h, g, w_up, w_down, layer, final_g):
    m, d = h.shape
    ff = w_up.shape[2]
    tm, tf = MLP_TM, MLP_TF
    final_norm = final_g is not None
    in_specs = [pl.BlockSpec((tm, d), lambda i, f: (i, 0)),
                pl.BlockSpec((1, d), lambda i, f: (0, 0)),
                pl.BlockSpec((None, d, tf), lambda i, f: (layer, 0, f)),
                pl.BlockSpec((None, tf, d), lambda i, f: (layer, f, 0))]
    args = [h, g.reshape(1, d), w_up, w_down]
    if final_norm:
        in_specs.append(pl.BlockSpec((1, d), lambda i, f: (0, 0)))
        args.append(final_g.reshape(1, d))
    return pl.pallas_call(
        functools.partial(_mlp_body, final_norm=final_norm),
        grid=(m // tm, ff // tf),
        in_specs=in_specs,
        out_specs=pl.BlockSpec((tm, d), lambda i, f: (i, 0)),
        out_shape=jax.ShapeDtypeStruct((m, d), F32),
        scratch_shapes=[pltpu.VMEM((tm, d), BF16)],
        compiler_params=_params(2),
        name="mlp",
    )(*args)


def _swa_body(relb_ref, sink_ref, bucket_ref, q_ref, kc_ref, kp_ref, vc_ref, vp_ref,
              o_ref, tab_ref):
    i = pl.program_id(0)
    hd, nq = HEAD_DIM_A, Q_BLOCK
    lanes = GROUP_A * nq

    @pl.when(i == 0)
    def _():
        bucket = bucket_ref[...]
        for head in range(N_HEADS_A):
            t = jnp.full((KEY_SPAN, nq), MASK_NEG, F32)
            for b in range(NUM_BUCKETS):
                t = jnp.where(bucket == b, relb_ref[b, head], t)
            tab_ref[:, head * nq:(head + 1) * nq] = t

    hist_pen = jnp.where(i == 0, MASK_NEG, 0.0).astype(F32)

    q_t = (q_ref[...] * (hd ** -0.5)).T
    k_all = jnp.concatenate([kp_ref[...], kc_ref[...]], axis=0).astype(BF16)
    v_t = jnp.concatenate([vp_ref[...].T, vc_ref[...].T], axis=1)
    zeros = jnp.zeros((hd, lanes), BF16)
    ones = jnp.ones((2 * V7X_SUBLANES, KEY_SPAN), BF16)

    kv_heads = range(N_KV_A)
    heads = [[h * GROUP_A + g for g in range(GROUP_A)] for h in kv_heads]
    rhs_rows = []
    for h in kv_heads:
        qg = jnp.concatenate([q_t[hd_ * hd:(hd_ + 1) * hd, :] for hd_ in heads[h]],
                             axis=1).astype(BF16)
        rhs_rows.append(jnp.concatenate([zeros] * h + [qg] + [zeros] * (N_KV_A - 1 - h), axis=1))
    logits = _dot(k_all, jnp.concatenate(rhs_rows, axis=0)) + tab_ref[...]
    logits = jnp.concatenate([logits[0:HIST] + hist_pen, logits[HIST:KEY_SPAN]], axis=0)
    sink = jnp.concatenate([jnp.full((1, nq), sink_ref[hd_], F32)
                            for h in kv_heads for hd_ in heads[h]], axis=1)
    m = jnp.maximum(jnp.max(logits, axis=0, keepdims=True), sink)
    p = jnp.exp(logits - m).astype(BF16)
    sink_p = jnp.exp(sink - m)
    for h in kv_heads:
        hs = slice(h * lanes, (h + 1) * lanes)
        v_aug = jnp.concatenate([v_t[h * hd:(h + 1) * hd, :].astype(BF16), ones], axis=0)
        acc = _dot(v_aug, p[:, hs])
        out_t = acc[0:hd, :] * (1.0 / (acc[hd:hd + 1, :] + sink_p[:, hs]))
        for g in range(0, GROUP_A, 2):
            both = jnp.concatenate([out_t[:, g * nq:(g + 1) * nq],
                                    out_t[:, (g + 1) * nq:(g + 2) * nq]], axis=0)
            c0 = heads[h][g] * hd
            o_ref[:, c0:c0 + 2 * hd] = both.T.astype(o_ref.dtype)


def _swa(proj0, rel_bias, sinks, bucket):
    s = proj0.shape[0]
    nb = s // Q_BLOCK
    k_blk, v_blk = OFF_K // KV_COLS, OFF_V // KV_COLS
    kv_spec = lambda blk, prev: pl.BlockSpec(
        (Q_BLOCK, KV_COLS),
        (lambda i: (jnp.maximum(i - 1, 0), blk)) if prev else (lambda i: (i, blk)))
    return pl.pallas_call(
        _swa_body,
        grid=(nb,),
        in_specs=[pl.BlockSpec(memory_space=pltpu.SMEM),
                  pl.BlockSpec(memory_space=pltpu.SMEM),
                  pl.BlockSpec((KEY_SPAN, Q_BLOCK), lambda i: (0, 0)),
                  pl.BlockSpec((Q_BLOCK, Q_COLS), lambda i: (i, 0)),
                  kv_spec(k_blk, False), kv_spec(k_blk, True),
                  kv_spec(v_blk, False), kv_spec(v_blk, True)],
        out_specs=pl.BlockSpec((Q_BLOCK, Q_COLS), lambda i: (i, 0)),
        out_shape=jax.ShapeDtypeStruct((s, Q_COLS), BF16),
        scratch_shapes=[pltpu.VMEM((KEY_SPAN, N_HEADS_A * Q_BLOCK), F32)],
        compiler_params=_params(1),
        name="swa_attention",
    )(rel_bias, sinks, bucket, proj0, proj0, proj0, proj0, proj0)


def _t5_bucket(rel):
    half = NUM_BUCKETS // 2
    ret = jnp.where(rel > 0, half, 0)
    n = jnp.abs(rel)
    max_exact = half // 2
    nf = jnp.maximum(n, 1).astype(F32)
    large = max_exact + (jnp.log(nf / max_exact) / math.log(MAX_DISTANCE / max_exact)
                         * (half - max_exact)).astype(jnp.int32)
    large = jnp.minimum(large, half - 1)
    return ret + jnp.where(n < max_exact, n, large)


def _swa_bucket_table():
    qi = jnp.arange(Q_BLOCK)
    sj = jnp.arange(KEY_SPAN)
    rel = (sj[:, None] - HIST) - qi[None, :]
    q_chunk = qi // CHUNK
    k_chunk = (sj - HIST) // CHUNK
    in_window = ((k_chunk[:, None] <= q_chunk[None, :])
                 & (k_chunk[:, None] >= q_chunk[None, :] - WINDOW_CHUNKS))
    return jnp.where(in_window, _t5_bucket(rel), -1).astype(jnp.int32)


LRU_PAD = V7X_SUBLANES
LRU_LANE_TILES = LRU_WIDTH // V7X_LANES
LRU_COL_BLOCKS = LRU_WIDTH // LRU_COL_BLOCK


def _neg_expm1(y, e):
    safe_log = jnp.where(e == 1.0, 1.0, jnp.log(e))
    near_zero = jnp.where(e == 1.0, -y, (1.0 - e) * y / safe_log)
    return jnp.where(y > -0.5, near_zero, 1.0 - e)


def _lru_body(*refs):
    xb_refs = refs[:LRU_COL_BLOCKS]
    gb_refs = refs[LRU_COL_BLOCKS:2 * LRU_COL_BLOCKS]
    (cw_ref, cb_ref, wa_ref, ba_ref, wx_ref, bx_ref, lam_ref,
     o_ref, xpad_ref, a_ref, u_ref, hs_ref, hc_ref) = refs[2 * LRU_COL_BLOCKS:]
    t = pl.program_id(0)
    tb, cb = LRU_TIME_BLOCK, LRU_WIDTH

    @pl.when(t == 0)
    def _():
        xpad_ref[0:LRU_PAD, :] = jnp.zeros((LRU_PAD, cb), F32)
        hc_ref[...] = jnp.zeros((1, cb), F32)

    x = jnp.concatenate([r[...] for r in xb_refs], axis=1)
    gb = jnp.concatenate([r[...] for r in gb_refs], axis=1)
    xpad_ref[LRU_PAD:LRU_PAD + tb, :] = x
    xc = cb_ref[...] + cw_ref[CONV_WIDTH - 1:CONV_WIDTH, :] * x
    for w in range(CONV_WIDTH - 1):
        off = LRU_PAD - (CONV_WIDTH - 1) + w
        xc = xc + cw_ref[w:w + 1, :] * xpad_ref[off:off + tb, :]
    xpad_ref[0:LRU_PAD, :] = x[tb - LRU_PAD:tb, :]

    neg_c_softplus = -LRU_C * jax.nn.softplus(-lam_ref[...])
    for c in range(LRU_LANE_TILES):
        cs = slice(c * V7X_LANES, (c + 1) * V7X_LANES)
        xcc = xc[:, cs]
        xcb = xcc.astype(BF16)
        r = jax.nn.sigmoid(_dot(xcb, wa_ref[c]) + ba_ref[:, cs])
        ig = jax.nn.sigmoid(_dot(xcb, wx_ref[c]) + bx_ref[:, cs])
        log_a = r * neg_c_softplus[:, cs]
        a = jnp.exp(log_a)
        a_ref[:, cs] = a
        u_ref[:, cs] = jnp.sqrt(_neg_expm1(2.0 * log_a, a * a)) * (ig * xcc)

    def scan_rows(r8, h):
        base = pl.multiple_of(r8 * V7X_SUBLANES, V7X_SUBLANES)
        for r in range(V7X_SUBLANES):
            h = a_ref[pl.ds(base + r, 1), :] * h + u_ref[pl.ds(base + r, 1), :]
            hs_ref[pl.ds(base + r, 1), :] = h
        return h

    hc_ref[...] = lax.fori_loop(0, tb // V7X_SUBLANES, scan_rows, hc_ref[...])
    o_ref[...] = (jax.nn.gelu(gb, approximate=True) * hs_ref[...]).astype(o_ref.dtype)


def _lru(proj0, conv_w, conv_b, wa_pairs, ba, wx_pairs, bx, lam):
    s = proj0.shape[0]
    tb, cb, wb = LRU_TIME_BLOCK, LRU_WIDTH, LRU_COL_BLOCK
    row = lambda v: v.reshape(1, cb)
    whole = lambda shape: pl.BlockSpec(shape, lambda t: (0,) * len(shape))
    col_blocks = lambda off: [
        pl.BlockSpec((tb, wb), functools.partial(lambda t, c: (t, c), c=off // wb + c))
        for c in range(LRU_COL_BLOCKS)]
    return pl.pallas_call(
        _lru_body,
        grid=(s // tb,),
        in_specs=col_blocks(OFF_X) + col_blocks(OFF_G) + [
            whole((CONV_WIDTH, cb)), whole((1, cb)),
            whole(wa_pairs.shape), whole((1, cb)),
            whole(wx_pairs.shape), whole((1, cb)), whole((1, cb))],
        out_specs=pl.BlockSpec((tb, cb), lambda t: (t, 0)),
        out_shape=jax.ShapeDtypeStruct((s, cb), BF16),
        scratch_shapes=[pltpu.VMEM((tb + LRU_PAD, cb), F32),
                        pltpu.VMEM((tb, cb), F32), pltpu.VMEM((tb, cb), F32),
                        pltpu.VMEM((tb, cb), F32), pltpu.VMEM((1, cb), F32)],
        compiler_params=_params(1),
        name="rg_lru",
    )(*([proj0] * (2 * LRU_COL_BLOCKS)), conv_w, row(conv_b), wa_pairs, row(ba),
      wx_pairs, row(bx), row(lam))


def _pair_block_diag(w):
    nb, bd, _ = w.shape
    wp = w.reshape(nb // 2, 2, bd, bd)
    z = jnp.zeros((nb // 2, bd, bd), w.dtype)
    top = jnp.concatenate([wp[:, 0], z], axis=2)
    bot = jnp.concatenate([z, wp[:, 1]], axis=2)
    return jnp.concatenate([top, bot], axis=1)


SB_LOG_F32_ZERO = -105.0


def _sb_logits(q, kb, mask):
    z = _dot_nt(q, kb) * (HEAD_DIM_C ** -0.5)
    sp = jnp.maximum(z, 0.0) + jnp.log(1.0 + jnp.exp2(jnp.abs(z) * -LOG2_E))
    if mask is not None:
        sp = jnp.where(mask, sp, 0.0)
    hi = sp.astype(BF16)
    lo = (sp - hi.astype(F32)).astype(BF16)
    return z, jnp.concatenate([hi, lo], axis=1)


def _sb_suffix_sums(hi_lo_list, tri2):
    rows = hi_lo_list[0].shape[0]
    out = _dot(jnp.concatenate(hi_lo_list, axis=0), tri2)
    return [out[t * rows:(t + 1) * rows] for t in range(len(hi_lo_list))]


def _sb_scores(q, kb, tri2, mask):
    z, hi_lo = _sb_logits(q, kb, mask)
    return z, _sb_suffix_sums([hi_lo], tri2)[0]


def _sb_weighted(z, incl, carry, vb, mask):
    w = jnp.exp(z - (incl + carry))
    if mask is not None:
        w = jnp.where(mask, w, 0.0)
    return _dot(w.astype(BF16), vb)


def _sb_body(q_ref, k_ref, v_ref, tri_prev_ref, tri_half_ref, o_ref):
    qi = pl.program_id(1)
    hd, nh = HEAD_DIM_C, SB_HALF
    heads = range(SB_HEADS_PER_STEP)
    tri_prev = tri_prev_ref[...]
    tri_half = tri_half_ref[...]
    row_id = lax.broadcasted_iota(jnp.int32, (nh, nh), 0)
    col_id = lax.broadcasted_iota(jnp.int32, (nh, nh), 1)
    before = col_id < row_id
    cols = lambda h: slice(h * hd, (h + 1) * hd)
    rows = lambda half: slice(half * nh, (half + 1) * nh)

    def kv(start, size, h):
        start = pl.multiple_of(start, nh)
        return k_ref[pl.ds(start, size), cols(h)], v_ref[pl.ds(start, size), cols(h)]

    @pl.when(qi == 0)
    def _first_block():
        for h in heads:
            k0, v0 = kv(0, nh, h)
            k1, v1 = kv(nh, nh, h)
            z, incl = _sb_scores(q_ref[rows(0), cols(h)], k0, tri_half, before)
            o_ref[rows(0), cols(h)] = _sb_weighted(z, incl, 0.0, v0, before).astype(o_ref.dtype)
            q_bot = q_ref[rows(1), cols(h)]
            zd, incl_d = _sb_scores(q_bot, k1, tri_half, before)
            zp, incl_p = _sb_scores(q_bot, k0, tri_half, None)
            acc = (_sb_weighted(zd, incl_d, 0.0, v1, before)
                   + _sb_weighted(zp, incl_p, incl_d[:, 0:1], v0, None))
            o_ref[rows(1), cols(h)] = acc.astype(o_ref.dtype)

    @pl.when(qi > 0)
    def _later_blocks():
        base = qi * SB_TQ
        tiles = [(h, half) for h in heads for half in range(SB_TQ // nh)]
        window_start = lambda half: base - SB_PREV + half * nh
        qs = [q_ref[rows(half), cols(h)] for h, half in tiles]
        kv_prev = [kv(window_start(half), SB_PREV, h) for h, half in tiles]
        kv_diag = [kv(window_start(half) + SB_PREV, nh, h) for h, half in tiles]
        l_diag = [_sb_logits(qs[t], kv_diag[t][0], before) for t in range(len(tiles))]
        l_prev = [_sb_logits(qs[t], kv_prev[t][0], None) for t in range(len(tiles))]
        i_diag = _sb_suffix_sums([hl for _, hl in l_diag], tri_half)
        i_prev = _sb_suffix_sums([hl for _, hl in l_prev], tri_prev)
        state = ()
        for t in range(len(tiles)):
            zd, incl_d, zp, incl_p = l_diag[t][0], i_diag[t], l_prev[t][0], i_prev[t]
            carry = incl_d[:, 0:1]
            acc = (_sb_weighted(zd, incl_d, 0.0, kv_diag[t][1], before)
                   + _sb_weighted(zp, incl_p, carry, kv_prev[t][1], None))
            state += (carry + incl_p[:, 0:1], acc)

        def live(loop_state):
            n, state = loop_state
            stick = jnp.min(functools.reduce(jnp.minimum, state[0::2]))
            more_keys = window_start(1) - (n + 1) * nh >= 0
            return jnp.logical_and(more_keys, -stick > SB_LOG_F32_ZERO)

        def body(loop_state):
            n, state = loop_state
            starts = [window_start(half) - (n + 1) * nh for _, half in tiles]
            kvs = [kv(jnp.maximum(starts[t], 0), nh, h) for t, (h, _) in enumerate(tiles)]
            logits = [_sb_logits(qs[t], kvs[t][0], None) for t in range(len(tiles))]
            incls = _sb_suffix_sums([hl for _, hl in logits], tri_half)
            out = ()
            for t in range(len(tiles)):
                valid = starts[t] >= 0
                carry, acc = state[2 * t], state[2 * t + 1]
                add = _sb_weighted(logits[t][0], incls[t], carry, kvs[t][1], None)
                out += (carry + jnp.where(valid, incls[t][:, 0:1], 0.0),
                        acc + jnp.where(valid, add, 0.0))
            return n + 1, out

        _, state = lax.while_loop(live, body, (jnp.int32(0), state))
        for t, (h, half) in enumerate(tiles):
            o_ref[rows(half), cols(h)] = state[2 * t + 1].astype(o_ref.dtype)


def _suffix_sum_matrix(n):
    jj = jnp.arange(n)
    tri = (jj[:, None] >= jj[None, :]).astype(BF16)
    return jnp.concatenate([tri, tri], axis=0)


def _stick_breaking(proj1):
    s = proj1.shape[0]
    hw = SB_HEADS_PER_STEP * HEAD_DIM_C
    n_groups = N_HEADS_C // SB_HEADS_PER_STEP
    whole = lambda shape: pl.BlockSpec(shape, lambda g, i: (0,) * len(shape))
    return pl.pallas_call(
        _sb_body,
        grid=(n_groups, s // SB_TQ),
        in_specs=[pl.BlockSpec((SB_TQ, hw), lambda g, i: (i, g)),
                  pl.BlockSpec((s, hw), lambda g, i: (0, n_groups + g)),
                  pl.BlockSpec((s, hw), lambda g, i: (0, 2 * n_groups + g)),
                  whole((2 * SB_PREV, SB_PREV)), whole((2 * SB_HALF, SB_HALF))],
        out_specs=pl.BlockSpec((SB_TQ, hw), lambda g, i: (i, g)),
        out_shape=jax.ShapeDtypeStruct((s, N_HEADS_C * HEAD_DIM_C), BF16),
        compiler_params=_params(2),
        name="stick_breaking",
    )(proj1, proj1, proj1, _suffix_sum_matrix(SB_PREV), _suffix_sum_matrix(SB_HALF))


def kernel(x, rel_bias, norm_mix, even_w_in, even_conv_w, even_conv_b, even_gate_a_w,
           even_gate_a_b, even_gate_x_w, even_gate_x_b, even_lru_lambda, even_sinks,
           even_w_out, odd_w_in, odd_w_out, norm_mlp, w_up, w_down, final_norm):
    b, s, d = x.shape
    h = x.reshape(b * s, d)

    proj0 = _norm_matmul(h, norm_mix[0], even_w_in[0], F32)
    y_a = _swa(proj0, rel_bias, even_sinks[0], _swa_bucket_table())
    y_b = _lru(proj0, even_conv_w[0], even_conv_b[0],
               _pair_block_diag(even_gate_a_w[0]).astype(BF16), even_gate_a_b[0],
               _pair_block_diag(even_gate_x_w[0]).astype(BF16), even_gate_x_b[0],
               even_lru_lambda[0])
    h = _matmul_res([y_a, y_b], even_w_out[0], h)
    h = _mlp(h, norm_mlp[0], w_up, w_down, 0, None)

    proj1 = _norm_matmul(h, norm_mix[1], odd_w_in[0], BF16)
    y = _stick_breaking(proj1)
    h = _matmul_res([y], odd_w_out[0], h)
    h = _mlp(h, norm_mlp[1], w_up, w_down, 1, final_norm)
    return h.reshape(b, s, d)
```

```python
import functools
import math

import jax
import jax.numpy as jnp
from jax import lax
from jax.experimental import pallas as pl
from jax.experimental.pallas import tpu as pltpu

F32 = jnp.float32
BF16 = jnp.bfloat16

CHUNK = 64
EPS = 1e-6

HEAD_DIM_A = 64
N_HEADS_A = 16
N_KV_A = 4
GROUP_A = N_HEADS_A // N_KV_A
WINDOW_CHUNKS = 2
Q_BLOCK = 128
HIST = WINDOW_CHUNKS * CHUNK
KEY_SPAN = Q_BLOCK + HIST
NUM_BUCKETS = 32
MAX_DISTANCE = 128

LRU_WIDTH = 1024
CONV_WIDTH = 4
LRU_C = 8.0

HEAD_DIM_C = 128
N_HEADS_C = 16

Q_COLS = N_HEADS_A * HEAD_DIM_A
KV_COLS = N_KV_A * HEAD_DIM_A
OFF_K = Q_COLS
OFF_V = OFF_K + KV_COLS
OFF_X = OFF_V + KV_COLS
OFF_G = OFF_X + LRU_WIDTH

V7X_LANES = 128
V7X_SUBLANES = 8
V7X_VMEM_LIMIT_BYTES = 56 * 1024 * 1024

MASK_NEG = -1e30
LOG2_E = math.log2(math.e)

PROJ_TM = 512
PROJ_TN_MAX = 1024
PROJ_LOAD_COLS = 256
PROJ_LOAD_SLOTS = 3
MLP_TM, MLP_TF = 1024, 512
SWA_BLOCKS_PER_STEP = 2
LRU_TIME_BLOCK = 512
LRU_COL_BLOCK = 512
SB_TQ = 256
SB_HALF = 128
SB_PREV = 256
SB_HEADS_PER_STEP = 4


def _params(n_grid_dims):
    return pltpu.CompilerParams(
        dimension_semantics=("arbitrary",) * n_grid_dims,
        vmem_limit_bytes=V7X_VMEM_LIMIT_BYTES)


def _proj_tn(n):
    tn = PROJ_TN_MAX
    while n % tn:
        tn //= 2
    return tn


def _rms_norm_f32(x, g):
    ms = jnp.mean(x * x, axis=-1, keepdims=True)
    return (x * lax.rsqrt(ms + EPS)) * g


def _dot(a, b):
    return jnp.dot(a, b, preferred_element_type=F32)


def _dot_nt(a, b):
    return lax.dot_general(a, b, (((1,), (1,)), ((), ())), preferred_element_type=F32)


def _resident(shape):
    return pl.BlockSpec(shape, lambda i: (0,) * len(shape), pipeline_mode=pl.Buffered(1))


def _weight_scratch(k, n):
    return [pltpu.VMEM((k, n), BF16),
            pltpu.VMEM((PROJ_LOAD_SLOTS, k, PROJ_LOAD_COLS), F32),
            pltpu.SemaphoreType.DMA((PROJ_LOAD_SLOTS,))]


def _for_each_column_chunk(w_hbm, w_ref, stage_ref, sem, tn, compute):
    n = w_ref.shape[1]
    lc, slots = PROJ_LOAD_COLS, PROJ_LOAD_SLOTS
    n_loads, loads_per_chunk = n // lc, tn // lc

    def copy(c):
        return pltpu.make_async_copy(w_hbm.at[:, pl.ds(c * lc, lc)], stage_ref.at[c % slots],
                                     sem.at[c % slots])

    @pl.when(pl.program_id(0) == 0)
    def _():
        for l in range(min(slots - 1, n_loads)):
            copy(l).start()
        for c in range(n // tn):
            for l in range(c * loads_per_chunk, (c + 1) * loads_per_chunk):
                if l + slots - 1 < n_loads:
                    copy(l + slots - 1).start()
                copy(l).wait()
                w_ref[:, l * lc:(l + 1) * lc] = stage_ref[l % slots].astype(BF16)
            compute(slice(c * tn, (c + 1) * tn))

    @pl.when(pl.program_id(0) > 0)
    def _():
        for c in range(n // tn):
            compute(slice(c * tn, (c + 1) * tn))


def _norm_matmul_body(x_ref, g_ref, w_hbm, o_ref, hn_ref, w_ref, stage_ref, sem):
    hn_ref[...] = _rms_norm_f32(x_ref[...], g_ref[...]).astype(BF16)

    def compute(cs):
        o_ref[:, cs] = _dot(hn_ref[...], w_ref[:, cs]).astype(o_ref.dtype)

    _for_each_column_chunk(w_hbm, w_ref, stage_ref, sem, _proj_tn(o_ref.shape[1]), compute)


def _norm_matmul(x, g, w, out_dtype):
    m, k = x.shape
    n = w.shape[1]
    tm = PROJ_TM
    return pl.pallas_call(
        _norm_matmul_body,
        grid=(m // tm,),
        in_specs=[pl.BlockSpec((tm, k), lambda i: (i, 0)),
                  _resident((1, k)), pl.BlockSpec(memory_space=pl.ANY)],
        out_specs=pl.BlockSpec((tm, n), lambda i: (i, 0)),
        out_shape=jax.ShapeDtypeStruct((m, n), out_dtype),
        scratch_shapes=[pltpu.VMEM((tm, k), BF16)] + _weight_scratch(k, n),
        compiler_params=_params(1),
        name="norm_matmul",
    )(x, g.reshape(1, k), w)


def _matmul_res_body(*refs, n_parts):
    a_refs = refs[:n_parts]
    w_hbm, res_ref, o_ref, w_ref, stage_ref, sem = refs[n_parts:]
    kp = w_ref.shape[0] // n_parts

    def compute(cs):
        acc = res_ref[:, cs]
        for p, a_ref in enumerate(a_refs):
            acc = acc + _dot(a_ref[...], w_ref[p * kp:(p + 1) * kp, cs])
        o_ref[:, cs] = acc

    _for_each_column_chunk(w_hbm, w_ref, stage_ref, sem, _proj_tn(o_ref.shape[1]), compute)


def _matmul_res(a_list, w, res):
    m, n = res.shape
    tm = PROJ_TM
    n_parts = len(a_list)
    kp = w.shape[0] // n_parts
    in_specs = [pl.BlockSpec((tm, kp), lambda i: (i, 0)) for _ in a_list]
    in_specs += [pl.BlockSpec(memory_space=pl.ANY), pl.BlockSpec((tm, n), lambda i: (i, 0))]
    return pl.pallas_call(
        functools.partial(_matmul_res_body, n_parts=n_parts),
        grid=(m // tm,),
        in_specs=in_specs,
        out_specs=pl.BlockSpec((tm, n), lambda i: (i, 0)),
        out_shape=jax.ShapeDtypeStruct((m, n), F32),
        scratch_shapes=_weight_scratch(w.shape[0], n),
        compiler_params=_params(1),
        name="matmul_res",
    )(*a_list, w, res)


def _mlp_body(*refs, final_norm):
    if final_norm:
        h_ref, g_ref, wu_ref, wd_ref, fg_ref, o_ref, hn_ref = refs
    else:
        h_ref, g_ref, wu_ref, wd_ref, o_ref, hn_ref = refs
    f = pl.program_id(1)

    @pl.when(f == 0)
    def _():
        h = h_ref[...]
        hn_ref[...] = _rms_norm_f32(h, g_ref[...]).astype(BF16)
        o_ref[...] = h

    up = _dot(hn_ref[...], wu_ref[...].astype(BF16))
    act = jnp.square(jnp.maximum(up, 0.0)).astype(BF16)
    o_ref[...] += _dot(act, wd_ref[...].astype(BF16))

    if final_norm:
        @pl.when(f == pl.num_programs(1) - 1)
        def _():
            o_ref[...] = _rms_norm_f32(o_ref[...], fg_ref[...])


def _mlp(h, g, w_up, w_down, layer, final_g):
    m, d = h.shape
    ff = w_up.shape[2]
    tm, tf = MLP_TM, MLP_TF
    final_norm = final_g is not None
    in_specs = [pl.BlockSpec((tm, d), lambda i, f: (i, 0)),
                pl.BlockSpec((1, d), lambda i, f: (0, 0)),
                pl.BlockSpec((None, d, tf), lambda i, f: (layer, 0, f)),
                pl.BlockSpec((None, tf, d), lambda i, f: (layer, f, 0))]
    args = [h, g.reshape(1, d), w_up, w_down]
    if final_norm:
        in_specs.append(pl.BlockSpec((1, d), lambda i, f: (0, 0)))
        args.append(final_g.reshape(1, d))
    return pl.pallas_call(
        functools.partial(_mlp_body, final_norm=final_norm),
        grid=(m // tm, ff // tf),
        in_specs=in_specs,
        out_specs=pl.BlockSpec((tm, d), lambda i, f: (i, 0)),
        out_shape=jax.ShapeDtypeStruct((m, d), F32),
        scratch_shapes=[pltpu.VMEM((tm, d), BF16)],
        compiler_params=_params(2),
        name="mlp",
    )(*args)


def _swa_body(relb_ref, sink_ref, bucket_ref, q_ref, kc_ref, kp_ref, vc_ref, vp_ref,
              o_ref, tab_ref):
    i = pl.program_id(0)
    nq = Q_BLOCK

    @pl.when(i == 0)
    def _():
        bucket = bucket_ref[...]
        for head in range(N_HEADS_A):
            t = jnp.full((KEY_SPAN, nq), MASK_NEG, F32)
            for b in range(NUM_BUCKETS):
                t = jnp.where(bucket == b, relb_ref[b, head], t)
            tab_ref[:, head * nq:(head + 1) * nq] = t

    k_prev, k_cur = kp_ref[...], kc_ref[...]
    v_prev, v_cur = vp_ref[...], vc_ref[...]
    first_pen = jnp.where(i == 0, MASK_NEG, 0.0).astype(F32)
    _swa_block(q_ref[0:nq, :], k_prev, k_cur[0:nq], v_prev, v_cur[0:nq], first_pen,
               tab_ref, sink_ref, o_ref.at[0:nq, :])
    _swa_block(q_ref[nq:2 * nq, :], k_cur[0:nq], k_cur[nq:2 * nq], v_cur[0:nq], v_cur[nq:2 * nq],
               None, tab_ref, sink_ref, o_ref.at[nq:2 * nq, :])


def _swa_block(q, k_hist, k_own, v_hist, v_own, hist_pen, tab_ref, sink_ref, o_ref):
    hd, nq = HEAD_DIM_A, Q_BLOCK
    lanes = GROUP_A * nq
    q_t = (q * (hd ** -0.5)).T
    k_all = jnp.concatenate([k_hist, k_own], axis=0).astype(BF16)
    v_t = jnp.concatenate([v_hist.T, v_own.T], axis=1)
    zeros = jnp.zeros((hd, lanes), BF16)
    ones = jnp.ones((2 * V7X_SUBLANES, KEY_SPAN), BF16)

    kv_heads = range(N_KV_A)
    heads = [[h * GROUP_A + g for g in range(GROUP_A)] for h in kv_heads]
    rhs_rows = []
    for h in kv_heads:
        qg = jnp.concatenate([q_t[hd_ * hd:(hd_ + 1) * hd, :] for hd_ in heads[h]],
                             axis=1).astype(BF16)
        rhs_rows.append(jnp.concatenate([zeros] * h + [qg] + [zeros] * (N_KV_A - 1 - h), axis=1))
    logits = _dot(k_all, jnp.concatenate(rhs_rows, axis=0)) + tab_ref[...]
    if hist_pen is not None:
        logits = jnp.concatenate([logits[0:HIST] + hist_pen, logits[HIST:KEY_SPAN]], axis=0)
    sink = jnp.concatenate([jnp.full((1, nq), sink_ref[hd_], F32)
                            for h in kv_heads for hd_ in heads[h]], axis=1)
    m = jnp.maximum(jnp.max(logits, axis=0, keepdims=True), sink)
    p = jnp.exp(logits - m).astype(BF16)
    sink_p = jnp.exp(sink - m)
    for h in kv_heads:
        hs = slice(h * lanes, (h + 1) * lanes)
        v_aug = jnp.concatenate([v_t[h * hd:(h + 1) * hd, :].astype(BF16), ones], axis=0)
        acc = _dot(v_aug, p[:, hs])
        out_t = acc[0:hd, :] * (1.0 / (acc[hd:hd + 1, :] + sink_p[:, hs]))
        for g in range(0, GROUP_A, 2):
            both = jnp.concatenate([out_t[:, g * nq:(g + 1) * nq],
                                    out_t[:, (g + 1) * nq:(g + 2) * nq]], axis=0)
            c0 = heads[h][g] * hd
            o_ref[:, c0:c0 + 2 * hd] = both.T.astype(o_ref.dtype)


def _swa(proj0, rel_bias, sinks, bucket):
    s = proj0.shape[0]
    rows = SWA_BLOCKS_PER_STEP * Q_BLOCK
    k_blk, v_blk = OFF_K // KV_COLS, OFF_V // KV_COLS
    own = lambda blk: pl.BlockSpec((rows, KV_COLS), lambda i: (i, blk))
    before = lambda blk: pl.BlockSpec(
        (Q_BLOCK, KV_COLS), lambda i: (jnp.maximum(SWA_BLOCKS_PER_STEP * i - 1, 0), blk))
    return pl.pallas_call(
        _swa_body,
        grid=(s // rows,),
        in_specs=[pl.BlockSpec(memory_space=pltpu.SMEM),
                  pl.BlockSpec(memory_space=pltpu.SMEM),
                  pl.BlockSpec((KEY_SPAN, Q_BLOCK), lambda i: (0, 0)),
                  pl.BlockSpec((rows, Q_COLS), lambda i: (i, 0)),
                  own(k_blk), before(k_blk), own(v_blk), before(v_blk)],
        out_specs=pl.BlockSpec((rows, Q_COLS), lambda i: (i, 0)),
        out_shape=jax.ShapeDtypeStruct((s, Q_COLS), BF16),
        scratch_shapes=[pltpu.VMEM((KEY_SPAN, N_HEADS_A * Q_BLOCK), F32)],
        compiler_params=_params(1),
        name="swa_attention",
    )(rel_bias, sinks, bucket, proj0, proj0, proj0, proj0, proj0)


def _t5_bucket(rel):
    half = NUM_BUCKETS // 2
    ret = jnp.where(rel > 0, half, 0)
    n = jnp.abs(rel)
    max_exact = half // 2
    nf = jnp.maximum(n, 1).astype(F32)
    large = max_exact + (jnp.log(nf / max_exact) / math.log(MAX_DISTANCE / max_exact)
                         * (half - max_exact)).astype(jnp.int32)
    large = jnp.minimum(large, half - 1)
    return ret + jnp.where(n < max_exact, n, large)


def _swa_bucket_table():
    qi = jnp.arange(Q_BLOCK)
    sj = jnp.arange(KEY_SPAN)
    rel = (sj[:, None] - HIST) - qi[None, :]
    q_chunk = qi // CHUNK
    k_chunk = (sj - HIST) // CHUNK
    in_window = ((k_chunk[:, None] <= q_chunk[None, :])
                 & (k_chunk[:, None] >= q_chunk[None, :] - WINDOW_CHUNKS))
    return jnp.where(in_window, _t5_bucket(rel), -1).astype(jnp.int32)


LRU_PAD = V7X_SUBLANES
LRU_LANE_TILES = LRU_WIDTH // V7X_LANES
LRU_COL_BLOCKS = LRU_WIDTH // LRU_COL_BLOCK


def _neg_expm1(y, e):
    safe_log = jnp.where(e == 1.0, 1.0, jnp.log(e))
    near_zero = jnp.where(e == 1.0, -y, (1.0 - e) * y / safe_log)
    return jnp.where(y > -0.5, near_zero, 1.0 - e)


def _lru_body(*refs):
    xb_refs = refs[:LRU_COL_BLOCKS]
    gb_refs = refs[LRU_COL_BLOCKS:2 * LRU_COL_BLOCKS]
    (cw_ref, cb_ref, wa_ref, ba_ref, wx_ref, bx_ref, lam_ref,
     o_ref, xpad_ref, a_ref, u_ref, hs_ref, hc_ref) = refs[2 * LRU_COL_BLOCKS:]
    t = pl.program_id(0)
    tb, cb = LRU_TIME_BLOCK, LRU_WIDTH

    @pl.when(t == 0)
    def _():
        xpad_ref[0:LRU_PAD, :] = jnp.zeros((LRU_PAD, cb), F32)
        hc_ref[...] = jnp.zeros((1, cb), F32)

    x = jnp.concatenate([r[...] for r in xb_refs], axis=1)
    gb = jnp.concatenate([r[...] for r in gb_refs], axis=1)
    xpad_ref[LRU_PAD:LRU_PAD + tb, :] = x
    xc = cb_ref[...] + cw_ref[CONV_WIDTH - 1:CONV_WIDTH, :] * x
    for w in range(CONV_WIDTH - 1):
        off = LRU_PAD - (CONV_WIDTH - 1) + w
        xc = xc + cw_ref[w:w + 1, :] * xpad_ref[off:off + tb, :]
    xpad_ref[0:LRU_PAD, :] = x[tb - LRU_PAD:tb, :]

    neg_c_softplus = -LRU_C * jax.nn.softplus(-lam_ref[...])
    for c in range(LRU_LANE_TILES):
        cs = slice(c * V7X_LANES, (c + 1) * V7X_LANES)
        xcc = xc[:, cs]
        xcb = xcc.astype(BF16)
        r = jax.nn.sigmoid(_dot(xcb, wa_ref[c]) + ba_ref[:, cs])
        ig = jax.nn.sigmoid(_dot(xcb, wx_ref[c]) + bx_ref[:, cs])
        log_a = r * neg_c_softplus[:, cs]
        a = jnp.exp(log_a)
        a_ref[:, cs] = a
        u_ref[:, cs] = jnp.sqrt(_neg_expm1(2.0 * log_a, a * a)) * (ig * xcc)

    def scan_rows(r8, h):
        base = pl.multiple_of(r8 * V7X_SUBLANES, V7X_SUBLANES)
        for r in range(V7X_SUBLANES):
            h = a_ref[pl.ds(base + r, 1), :] * h + u_ref[pl.ds(base + r, 1), :]
            hs_ref[pl.ds(base + r, 1), :] = h
        return h

    hc_ref[...] = lax.fori_loop(0, tb // V7X_SUBLANES, scan_rows, hc_ref[...])
    o_ref[...] = (jax.nn.gelu(gb, approximate=True) * hs_ref[...]).astype(o_ref.dtype)


def _lru(proj0, conv_w, conv_b, wa_pairs, ba, wx_pairs, bx, lam):
    s = proj0.shape[0]
    tb, cb, wb = LRU_TIME_BLOCK, LRU_WIDTH, LRU_COL_BLOCK
    row = lambda v: v.reshape(1, cb)
    whole = lambda shape: pl.BlockSpec(shape, lambda t: (0,) * len(shape))
    col_blocks = lambda off: [
        pl.BlockSpec((tb, wb), functools.partial(lambda t, c: (t, c), c=off // wb + c))
        for c in range(LRU_COL_BLOCKS)]
    return pl.pallas_call(
        _lru_body,
        grid=(s // tb,),
        in_specs=col_blocks(OFF_X) + col_blocks(OFF_G) + [
            whole((CONV_WIDTH, cb)), whole((1, cb)),
            whole(wa_pairs.shape), whole((1, cb)),
            whole(wx_pairs.shape), whole((1, cb)), whole((1, cb))],
        out_specs=pl.BlockSpec((tb, cb), lambda t: (t, 0)),
        out_shape=jax.ShapeDtypeStruct((s, cb), BF16),
        scratch_shapes=[pltpu.VMEM((tb + LRU_PAD, cb), F32),
                        pltpu.VMEM((tb, cb), F32), pltpu.VMEM((tb, cb), F32),
                        pltpu.VMEM((tb, cb), F32), pltpu.VMEM((1, cb), F32)],
        compiler_params=_params(1),
        name="rg_lru",
    )(*([proj0] * (2 * LRU_COL_BLOCKS)), conv_w, row(conv_b), wa_pairs, row(ba),
      wx_pairs, row(bx), row(lam))


def _pair_block_diag(w):
    nb, bd, _ = w.shape
    wp = w.reshape(nb // 2, 2, bd, bd)
    z = jnp.zeros((nb // 2, bd, bd), w.dtype)
    top = jnp.concatenate([wp[:, 0], z], axis=2)
    bot = jnp.concatenate([z, wp[:, 1]], axis=2)
    return jnp.concatenate([top, bot], axis=1)


SB_LOG_F32_ZERO = -105.0


def _sb_logits(q, kb, mask):
    z = _dot_nt(q, kb) * (HEAD_DIM_C ** -0.5)
    sp = jnp.maximum(z, 0.0) + jnp.log(1.0 + jnp.exp2(jnp.abs(z) * -LOG2_E))
    if mask is not None:
        sp = jnp.where(mask, sp, 0.0)
    hi = sp.astype(BF16)
    lo = (sp - hi.astype(F32)).astype(BF16)
    return z, jnp.concatenate([hi, lo], axis=1)


def _sb_suffix_sums(hi_lo_list, tri2):
    rows = hi_lo_list[0].shape[0]
    out = _dot(jnp.concatenate(hi_lo_list, axis=0), tri2)
    return [out[t * rows:(t + 1) * rows] for t in range(len(hi_lo_list))]


def _sb_scores(q, kb, tri2, mask):
    z, hi_lo = _sb_logits(q, kb, mask)
    return z, _sb_suffix_sums([hi_lo], tri2)[0]


def _sb_weighted(z, incl, carry, vb, mask):
    w = jnp.exp(z - (incl + carry))
    if mask is not None:
        w = jnp.where(mask, w, 0.0)
    return _dot(w.astype(BF16), vb)


def _sb_body(q_ref, k_ref, v_ref, tri_prev_ref, tri_half_ref, o_ref):
    qi = pl.program_id(1)
    hd, nh = HEAD_DIM_C, SB_HALF
    heads = range(SB_HEADS_PER_STEP)
    tri_prev = tri_prev_ref[...]
    tri_half = tri_half_ref[...]
    row_id = lax.broadcasted_iota(jnp.int32, (nh, nh), 0)
    col_id = lax.broadcasted_iota(jnp.int32, (nh, nh), 1)
    before = col_id < row_id
    cols = lambda h: slice(h * hd, (h + 1) * hd)
    rows = lambda half: slice(half * nh, (half + 1) * nh)

    def kv(start, size, h):
        start = pl.multiple_of(start, nh)
        return k_ref[pl.ds(start, size), cols(h)], v_ref[pl.ds(start, size), cols(h)]

    @pl.when(qi == 0)
    def _first_block():
        for h in heads:
            k0, v0 = kv(0, nh, h)
            k1, v1 = kv(nh, nh, h)
            z, incl = _sb_scores(q_ref[rows(0), cols(h)], k0, tri_half, before)
            o_ref[rows(0), cols(h)] = _sb_weighted(z, incl, 0.0, v0, before).astype(o_ref.dtype)
            q_bot = q_ref[rows(1), cols(h)]
            zd, incl_d = _sb_scores(q_bot, k1, tri_half, before)
            zp, incl_p = _sb_scores(q_bot, k0, tri_half, None)
            acc = (_sb_weighted(zd, incl_d, 0.0, v1, before)
                   + _sb_weighted(zp, incl_p, incl_d[:, 0:1], v0, None))
            o_ref[rows(1), cols(h)] = acc.astype(o_ref.dtype)

    @pl.when(qi > 0)
    def _later_blocks():
        base = qi * SB_TQ
        tiles = [(h, half) for h in heads for half in range(SB_TQ // nh)]
        window_start = lambda half: base - SB_PREV + half * nh
        qs = [q_ref[rows(half), cols(h)] for h, half in tiles]
        kv_prev = [kv(window_start(half), SB_PREV, h) for h, half in tiles]
        kv_diag = [kv(window_start(half) + SB_PREV, nh, h) for h, half in tiles]
        l_diag = [_sb_logits(qs[t], kv_diag[t][0], before) for t in range(len(tiles))]
        l_prev = [_sb_logits(qs[t], kv_prev[t][0], None) for t in range(len(tiles))]
        i_diag = _sb_suffix_sums([hl for _, hl in l_diag], tri_half)
        i_prev = _sb_suffix_sums([hl for _, hl in l_prev], tri_prev)
        state = ()
        for t in range(len(tiles)):
            zd, incl_d, zp, incl_p = l_diag[t][0], i_diag[t], l_prev[t][0], i_prev[t]
            carry = incl_d[:, 0:1]
            acc = (_sb_weighted(zd, incl_d, 0.0, kv_diag[t][1], before)
                   + _sb_weighted(zp, incl_p, carry, kv_prev[t][1], None))
            state += (carry + incl_p[:, 0:1], acc)

        def live(loop_state):
            n, state = loop_state
            stick = jnp.min(functools.reduce(jnp.minimum, state[0::2]))
            more_keys = window_start(1) - (n + 1) * nh >= 0
            return jnp.logical_and(more_keys, -stick > SB_LOG_F32_ZERO)

        def body(loop_state):
            n, state = loop_state
            starts = [window_start(half) - (n + 1) * nh for _, half in tiles]
            kvs = [kv(jnp.maximum(starts[t], 0), nh, h) for t, (h, _) in enumerate(tiles)]
            logits = [_sb_logits(qs[t], kvs[t][0], None) for t in range(len(tiles))]
            incls = _sb_suffix_sums([hl for _, hl in logits], tri_half)
            out = ()
            for t in range(len(tiles)):
                valid = starts[t] >= 0
                carry, acc = state[2 * t], state[2 * t + 1]
                add = _sb_weighted(logits[t][0], incls[t], carry, kvs[t][1], None)
                out += (carry + jnp.where(valid, incls[t][:, 0:1], 0.0),
                        acc + jnp.where(valid, add, 0.0))
            return n + 1, out

        _, state = lax.while_loop(live, body, (jnp.int32(0), state))
        for t, (h, half) in enumerate(tiles):
            o_ref[rows(half), cols(h)] = state[2 * t + 1].astype(o_ref.dtype)


def _suffix_sum_matrix(n):
    jj = jnp.arange(n)
    tri = (jj[:, None] >= jj[None, :]).astype(BF16)
    return jnp.concatenate([tri, tri], axis=0)


def _stick_breaking(proj1):
    s = proj1.shape[0]
    hw = SB_HEADS_PER_STEP * HEAD_DIM_C
    n_groups = N_HEADS_C // SB_HEADS_PER_STEP
    whole = lambda shape: pl.BlockSpec(shape, lambda g, i: (0,) * len(shape))
    return pl.pallas_call(
        _sb_body,
        grid=(n_groups, s // SB_TQ),
        in_specs=[pl.BlockSpec((SB_TQ, hw), lambda g, i: (i, g)),
                  pl.BlockSpec((s, hw), lambda g, i: (0, n_groups + g)),
                  pl.BlockSpec((s, hw), lambda g, i: (0, 2 * n_groups + g)),
                  whole((2 * SB_PREV, SB_PREV)), whole((2 * SB_HALF, SB_HALF))],
        out_specs=pl.BlockSpec((SB_TQ, hw), lambda g, i: (i, g)),
        out_shape=jax.ShapeDtypeStruct((s, N_HEADS_C * HEAD_DIM_C), BF16),
        compiler_params=_params(2),
        name="stick_breaking",
    )(proj1, proj1, proj1, _suffix_sum_matrix(SB_PREV), _suffix_sum_matrix(SB_HALF))


def kernel(x, rel_bias, norm_mix, even_w_in, even_conv_w, even_conv_b, even_gate_a_w,
           even_gate_a_b, even_gate_x_w, even_gate_x_b, even_lru_lambda, even_sinks,
           even_w_out, odd_w_in, odd_w_out, norm_mlp, w_up, w_down, final_norm):
    b, s, d = x.shape
    h = x.reshape(b * s, d)

    proj0 = _norm_matmul(h, norm_mix[0], even_w_in[0], F32)
    y_a = _swa(proj0, rel_bias, even_sinks[0], _swa_bucket_table())
    y_b = _lru(proj0, even_conv_w[0], even_conv_b[0],
               _pair_block_diag(even_gate_a_w[0]).astype(BF16), even_gate_a_b[0],
               _pair_block_diag(even_gate_x_w[0]).astype(BF16), even_gate_x_b[0],
               even_lru_lambda[0])
    h = _matmul_res([y_a, y_b], even_w_out[0], h)
    h = _mlp(h, norm_mlp[0], w_up, w_down, 0, None)

    proj1 = _norm_matmul(h, norm_mix[1], odd_w_in[0], BF16)
    y = _stick_breaking(proj1)
    h = _matmul_res([y], odd_w_out[0], h)
    h = _mlp(h, norm_mlp[1], w_up, w_down, 1, final_norm)
    return h.reshape(b, s, d)
```

```python
import functools
import math

import jax
import jax.numpy as jnp
from jax import lax
from jax.experimental import pallas as pl
from jax.experimental.pallas import tpu as pltpu

F32 = jnp.float32
BF16 = jnp.bfloat16

CHUNK = 64
EPS = 1e-6

HEAD_DIM_A = 64
N_HEADS_A = 16
N_KV_A = 4
GROUP_A = N_HEADS_A // N_KV_A
WINDOW_CHUNKS = 2
Q_BLOCK = 128
HIST = WINDOW_CHUNKS * CHUNK
KEY_SPAN = Q_BLOCK + HIST
NUM_BUCKETS = 32
MAX_DISTANCE = 128

LRU_WIDTH = 1024
CONV_WIDTH = 4
LRU_C = 8.0

HEAD_DIM_C = 128
N_HEADS_C = 16

Q_COLS = N_HEADS_A * HEAD_DIM_A
KV_COLS = N_KV_A * HEAD_DIM_A
OFF_K = Q_COLS
OFF_V = OFF_K + KV_COLS
OFF_X = OFF_V + KV_COLS
OFF_G = OFF_X + LRU_WIDTH

V7X_LANES = 128
V7X_SUBLANES = 8
V7X_VMEM_LIMIT_BYTES = 56 * 1024 * 1024

MASK_NEG = -1e30
LOG2_E = math.log2(math.e)

PROJ_TM = 512
PROJ_TN_MAX = 1024
PROJ_LOAD_COLS = 256
PROJ_LOAD_SLOTS = 3
MLP_TM, MLP_TF = 1024, 512
SWA_BLOCKS_PER_STEP = 8
LRU_TIME_BLOCK = 1024
LRU_COL_BLOCK = 512
SB_TQ = 256
SB_HALF = 128
SB_PREV = 256
SB_HEADS_PER_STEP = 4


def _params(n_grid_dims):
    return pltpu.CompilerParams(
        dimension_semantics=("arbitrary",) * n_grid_dims,
        vmem_limit_bytes=V7X_VMEM_LIMIT_BYTES)


def _proj_tn(n):
    tn = PROJ_TN_MAX
    while n % tn:
        tn //= 2
    return tn


def _rms_norm_f32(x, g):
    ms = jnp.mean(x * x, axis=-1, keepdims=True)
    return (x * lax.rsqrt(ms + EPS)) * g


def _dot(a, b):
    return jnp.dot(a, b, preferred_element_type=F32)


def _dot_nt(a, b):
    return lax.dot_general(a, b, (((1,), (1,)), ((), ())), preferred_element_type=F32)


def _resident(shape):
    return pl.BlockSpec(shape, lambda i: (0,) * len(shape), pipeline_mode=pl.Buffered(1))


def _weight_scratch(k, n):
    return [pltpu.VMEM((k, n), BF16),
            pltpu.VMEM((PROJ_LOAD_SLOTS, k, PROJ_LOAD_COLS), F32),
            pltpu.SemaphoreType.DMA((PROJ_LOAD_SLOTS,))]


def _for_each_column_chunk(w_hbm, w_ref, stage_ref, sem, tn, compute):
    n = w_ref.shape[1]
    lc, slots = PROJ_LOAD_COLS, PROJ_LOAD_SLOTS
    n_loads, loads_per_chunk = n // lc, tn // lc

    def copy(c):
        return pltpu.make_async_copy(w_hbm.at[:, pl.ds(c * lc, lc)], stage_ref.at[c % slots],
                                     sem.at[c % slots])

    @pl.when(pl.program_id(0) == 0)
    def _():
        for l in range(min(slots - 1, n_loads)):
            copy(l).start()
        for c in range(n // tn):
            for l in range(c * loads_per_chunk, (c + 1) * loads_per_chunk):
                if l + slots - 1 < n_loads:
                    copy(l + slots - 1).start()
                copy(l).wait()
                w_ref[:, l * lc:(l + 1) * lc] = stage_ref[l % slots].astype(BF16)
            compute(slice(c * tn, (c + 1) * tn))

    @pl.when(pl.program_id(0) > 0)
    def _():
        for c in range(n // tn):
            compute(slice(c * tn, (c + 1) * tn))


def _norm_matmul_body(x_ref, g_ref, w_hbm, o_ref, hn_ref, w_ref, stage_ref, sem):
    hn_ref[...] = _rms_norm_f32(x_ref[...], g_ref[...]).astype(BF16)

    def compute(cs):
        o_ref[:, cs] = _dot(hn_ref[...], w_ref[:, cs]).astype(o_ref.dtype)

    _for_each_column_chunk(w_hbm, w_ref, stage_ref, sem, _proj_tn(o_ref.shape[1]), compute)


def _norm_matmul(x, g, w, out_dtype):
    m, k = x.shape
    n = w.shape[1]
    tm = PROJ_TM
    return pl.pallas_call(
        _norm_matmul_body,
        grid=(m // tm,),
        in_specs=[pl.BlockSpec((tm, k), lambda i: (i, 0)),
                  _resident((1, k)), pl.BlockSpec(memory_space=pl.ANY)],
        out_specs=pl.BlockSpec((tm, n), lambda i: (i, 0)),
        out_shape=jax.ShapeDtypeStruct((m, n), out_dtype),
        scratch_shapes=[pltpu.VMEM((tm, k), BF16)] + _weight_scratch(k, n),
        compiler_params=_params(1),
        name="norm_matmul",
    )(x, g.reshape(1, k), w)


def _matmul_res_body(*refs, n_parts):
    a_refs = refs[:n_parts]
    w_hbm, res_ref, o_ref, w_ref, stage_ref, sem = refs[n_parts:]
    kp = w_ref.shape[0] // n_parts

    def compute(cs):
        acc = res_ref[:, cs]
        for p, a_ref in enumerate(a_refs):
            acc = acc + _dot(a_ref[...], w_ref[p * kp:(p + 1) * kp, cs])
        o_ref[:, cs] = acc

    _for_each_column_chunk(w_hbm, w_ref, stage_ref, sem, _proj_tn(o_ref.shape[1]), compute)


def _matmul_res(a_list, w, res):
    m, n = res.shape
    tm = PROJ_TM
    n_parts = len(a_list)
    kp = w.shape[0] // n_parts
    in_specs = [pl.BlockSpec((tm, kp), lambda i: (i, 0)) for _ in a_list]
    in_specs += [pl.BlockSpec(memory_space=pl.ANY), pl.BlockSpec((tm, n), lambda i: (i, 0))]
    return pl.pallas_call(
        functools.partial(_matmul_res_body, n_parts=n_parts),
        grid=(m // tm,),
        in_specs=in_specs,
        out_specs=pl.BlockSpec((tm, n), lambda i: (i, 0)),
        out_shape=jax.ShapeDtypeStruct((m, n), F32),
        scratch_shapes=_weight_scratch(w.shape[0], n),
        compiler_params=_params(1),
        name="matmul_res",
    )(*a_list, w, res)


def _mlp_body(*refs, final_norm):
    if final_norm:
        h_ref, g_ref, wu_ref, wd_ref, fg_ref, o_ref, hn_ref = refs
    else:
        h_ref, g_ref, wu_ref, wd_ref, o_ref, hn_ref = refs
    f = pl.program_id(1)

    @pl.when(f == 0)
    def _():
        h = h_ref[...]
        hn_ref[...] = _rms_norm_f32(h, g_ref[...]).astype(BF16)
        o_ref[...] = h

    up = _dot(hn_ref[...], wu_ref[...].astype(BF16))
    act = jnp.square(jnp.maximum(up, 0.0)).astype(BF16)
    o_ref[...] += _dot(act, wd_ref[...].astype(BF16))

    if final_norm:
        @pl.when(f == pl.num_programs(1) - 1)
        def _():
            o_ref[...] = _rms_norm_f32(o_ref[...], fg_ref[...])


def _mlp(h, g, w_up, w_down, layer, final_g):
    m, d = h.shape
    ff = w_up.shape[2]
    tm, tf = MLP_TM, MLP_TF
    final_norm = final_g is not None
    in_specs = [pl.BlockSpec((tm, d), lambda i, f: (i, 0)),
                pl.BlockSpec((1, d), lambda i, f: (0, 0)),
                pl.BlockSpec((None, d, tf), lambda i, f: (layer, 0, f)),
                pl.BlockSpec((None, tf, d), lambda i, f: (layer, f, 0))]
    args = [h, g.reshape(1, d), w_up, w_down]
    if final_norm:
        in_specs.append(pl.BlockSpec((1, d), lambda i, f: (0, 0)))
        args.append(final_g.reshape(1, d))
    return pl.pallas_call(
        functools.partial(_mlp_body, final_norm=final_norm),
        grid=(m // tm, ff // tf),
        in_specs=in_specs,
        out_specs=pl.BlockSpec((tm, d), lambda i, f: (i, 0)),
        out_shape=jax.ShapeDtypeStruct((m, d), F32),
        scratch_shapes=[pltpu.VMEM((tm, d), BF16)],
        compiler_params=_params(2),
        name="mlp",
    )(*args)


def _swa_body(relb_ref, sink_ref, bucket_ref, q_ref, kc_ref, kp_ref, vc_ref, vp_ref,
              o_ref, tab_ref):
    i = pl.program_id(0)
    nq = Q_BLOCK

    @pl.when(i == 0)
    def _():
        bucket = bucket_ref[...]
        for head in range(N_HEADS_A):
            t = jnp.full((KEY_SPAN, nq), MASK_NEG, F32)
            for b in range(NUM_BUCKETS):
                t = jnp.where(bucket == b, relb_ref[b, head], t)
            tab_ref[:, head * nq:(head + 1) * nq] = t

    k_cur, v_cur = kc_ref[...], vc_ref[...]
    first_pen = jnp.where(i == 0, MASK_NEG, 0.0).astype(F32)
    for b in range(SWA_BLOCKS_PER_STEP):
        own = slice(b * nq, (b + 1) * nq)
        if b == 0:
            k_hist, v_hist, pen = kp_ref[...], vp_ref[...], first_pen
        else:
            hist = slice((b - 1) * nq, b * nq)
            k_hist, v_hist, pen = k_cur[hist], v_cur[hist], None
        _swa_block(q_ref[own, :], k_hist, k_cur[own], v_hist, v_cur[own], pen,
                   tab_ref, sink_ref, o_ref.at[own, :])


def _swa_block(q, k_hist, k_own, v_hist, v_own, hist_pen, tab_ref, sink_ref, o_ref):
    hd, nq = HEAD_DIM_A, Q_BLOCK
    lanes = GROUP_A * nq
    q_t = (q * (hd ** -0.5)).T
    k_all = jnp.concatenate([k_hist, k_own], axis=0).astype(BF16)
    v_t = jnp.concatenate([v_hist.T, v_own.T], axis=1)
    zeros = jnp.zeros((hd, lanes), BF16)
    ones = jnp.ones((2 * V7X_SUBLANES, KEY_SPAN), BF16)

    kv_heads = range(N_KV_A)
    heads = [[h * GROUP_A + g for g in range(GROUP_A)] for h in kv_heads]
    rhs_rows = []
    for h in kv_heads:
        qg = jnp.concatenate([q_t[hd_ * hd:(hd_ + 1) * hd, :] for hd_ in heads[h]],
                             axis=1).astype(BF16)
        rhs_rows.append(jnp.concatenate([zeros] * h + [qg] + [zeros] * (N_KV_A - 1 - h), axis=1))
    logits = _dot(k_all, jnp.concatenate(rhs_rows, axis=0)) + tab_ref[...]
    if hist_pen is not None:
        logits = jnp.concatenate([logits[0:HIST] + hist_pen, logits[HIST:KEY_SPAN]], axis=0)
    sink = jnp.concatenate([jnp.full((1, nq), sink_ref[hd_], F32)
                            for h in kv_heads for hd_ in heads[h]], axis=1)
    m = jnp.maximum(jnp.max(logits, axis=0, keepdims=True), sink)
    p = jnp.exp(logits - m).astype(BF16)
    sink_p = jnp.exp(sink - m)
    for h in kv_heads:
        hs = slice(h * lanes, (h + 1) * lanes)
        v_aug = jnp.concatenate([v_t[h * hd:(h + 1) * hd, :].astype(BF16), ones], axis=0)
        acc = _dot(v_aug, p[:, hs])
        out_t = acc[0:hd, :] * (1.0 / (acc[hd:hd + 1, :] + sink_p[:, hs]))
        for g in range(0, GROUP_A, 2):
            both = jnp.concatenate([out_t[:, g * nq:(g + 1) * nq],
                                    out_t[:, (g + 1) * nq:(g + 2) * nq]], axis=0)
            c0 = heads[h][g] * hd
            o_ref[:, c0:c0 + 2 * hd] = both.T.astype(o_ref.dtype)


def _swa(proj0, rel_bias, sinks, bucket):
    s = proj0.shape[0]
    rows = SWA_BLOCKS_PER_STEP * Q_BLOCK
    k_blk, v_blk = OFF_K // KV_COLS, OFF_V // KV_COLS
    own = lambda blk: pl.BlockSpec((rows, KV_COLS), lambda i: (i, blk))
    before = lambda blk: pl.BlockSpec(
        (Q_BLOCK, KV_COLS), lambda i: (jnp.maximum(SWA_BLOCKS_PER_STEP * i - 1, 0), blk))
    return pl.pallas_call(
        _swa_body,
        grid=(s // rows,),
        in_specs=[pl.BlockSpec(memory_space=pltpu.SMEM),
                  pl.BlockSpec(memory_space=pltpu.SMEM),
                  pl.BlockSpec((KEY_SPAN, Q_BLOCK), lambda i: (0, 0)),
                  pl.BlockSpec((rows, Q_COLS), lambda i: (i, 0)),
                  own(k_blk), before(k_blk), own(v_blk), before(v_blk)],
        out_specs=pl.BlockSpec((rows, Q_COLS), lambda i: (i, 0)),
        out_shape=jax.ShapeDtypeStruct((s, Q_COLS), BF16),
        scratch_shapes=[pltpu.VMEM((KEY_SPAN, N_HEADS_A * Q_BLOCK), F32)],
        compiler_params=_params(1),
        name="swa_attention",
    )(rel_bias, sinks, bucket, proj0, proj0, proj0, proj0, proj0)


def _t5_bucket(rel):
    half = NUM_BUCKETS // 2
    ret = jnp.where(rel > 0, half, 0)
    n = jnp.abs(rel)
    max_exact = half // 2
    nf = jnp.maximum(n, 1).astype(F32)
    large = max_exact + (jnp.log(nf / max_exact) / math.log(MAX_DISTANCE / max_exact)
                         * (half - max_exact)).astype(jnp.int32)
    large = jnp.minimum(large, half - 1)
    return ret + jnp.where(n < max_exact, n, large)


def _swa_bucket_table():
    qi = jnp.arange(Q_BLOCK)
    sj = jnp.arange(KEY_SPAN)
    rel = (sj[:, None] - HIST) - qi[None, :]
    q_chunk = qi // CHUNK
    k_chunk = (sj - HIST) // CHUNK
    in_window = ((k_chunk[:, None] <= q_chunk[None, :])
                 & (k_chunk[:, None] >= q_chunk[None, :] - WINDOW_CHUNKS))
    return jnp.where(in_window, _t5_bucket(rel), -1).astype(jnp.int32)


LRU_PAD = V7X_SUBLANES
LRU_LANE_TILES = LRU_WIDTH // V7X_LANES
LRU_COL_BLOCKS = LRU_WIDTH // LRU_COL_BLOCK


def _neg_expm1(y, e):
    safe_log = jnp.where(e == 1.0, 1.0, jnp.log(e))
    near_zero = jnp.where(e == 1.0, -y, (1.0 - e) * y / safe_log)
    return jnp.where(y > -0.5, near_zero, 1.0 - e)


def _lru_body(*refs):
    xb_refs = refs[:LRU_COL_BLOCKS]
    gb_refs = refs[LRU_COL_BLOCKS:2 * LRU_COL_BLOCKS]
    (cw_ref, cb_ref, wa_ref, ba_ref, wx_ref, bx_ref, lam_ref,
     o_ref, xpad_ref, a_ref, u_ref, hs_ref, hc_ref) = refs[2 * LRU_COL_BLOCKS:]
    t = pl.program_id(0)
    tb, cb = LRU_TIME_BLOCK, LRU_WIDTH

    @pl.when(t == 0)
    def _():
        xpad_ref[0:LRU_PAD, :] = jnp.zeros((LRU_PAD, cb), F32)
        hc_ref[...] = jnp.zeros((1, cb), F32)

    x = jnp.concatenate([r[...] for r in xb_refs], axis=1)
    gb = jnp.concatenate([r[...] for r in gb_refs], axis=1)
    xpad_ref[LRU_PAD:LRU_PAD + tb, :] = x
    xc = cb_ref[...] + cw_ref[CONV_WIDTH - 1:CONV_WIDTH, :] * x
    for w in range(CONV_WIDTH - 1):
        off = LRU_PAD - (CONV_WIDTH - 1) + w
        xc = xc + cw_ref[w:w + 1, :] * xpad_ref[off:off + tb, :]
    xpad_ref[0:LRU_PAD, :] = x[tb - LRU_PAD:tb, :]

    neg_c_softplus = -LRU_C * jax.nn.softplus(-lam_ref[...])
    for c in range(LRU_LANE_TILES):
        cs = slice(c * V7X_LANES, (c + 1) * V7X_LANES)
        xcc = xc[:, cs]
        xcb = xcc.astype(BF16)
        r = jax.nn.sigmoid(_dot(xcb, wa_ref[c]) + ba_ref[:, cs])
        ig = jax.nn.sigmoid(_dot(xcb, wx_ref[c]) + bx_ref[:, cs])
        log_a = r * neg_c_softplus[:, cs]
        a = jnp.exp(log_a)
        a_ref[:, cs] = a
        u_ref[:, cs] = jnp.sqrt(_neg_expm1(2.0 * log_a, a * a)) * (ig * xcc)

    def scan_rows(r8, h):
        base = pl.multiple_of(r8 * V7X_SUBLANES, V7X_SUBLANES)
        for r in range(V7X_SUBLANES):
            h = a_ref[pl.ds(base + r, 1), :] * h + u_ref[pl.ds(base + r, 1), :]
            hs_ref[pl.ds(base + r, 1), :] = h
        return h

    hc_ref[...] = lax.fori_loop(0, tb // V7X_SUBLANES, scan_rows, hc_ref[...])
    o_ref[...] = (jax.nn.gelu(gb, approximate=True) * hs_ref[...]).astype(o_ref.dtype)


def _lru(proj0, conv_w, conv_b, wa_pairs, ba, wx_pairs, bx, lam):
    s = proj0.shape[0]
    tb, cb, wb = LRU_TIME_BLOCK, LRU_WIDTH, LRU_COL_BLOCK
    row = lambda v: v.reshape(1, cb)
    whole = lambda shape: pl.BlockSpec(shape, lambda t: (0,) * len(shape))
    col_blocks = lambda off: [
        pl.BlockSpec((tb, wb), functools.partial(lambda t, c: (t, c), c=off // wb + c))
        for c in range(LRU_COL_BLOCKS)]
    return pl.pallas_call(
        _lru_body,
        grid=(s // tb,),
        in_specs=col_blocks(OFF_X) + col_blocks(OFF_G) + [
            whole((CONV_WIDTH, cb)), whole((1, cb)),
            whole(wa_pairs.shape), whole((1, cb)),
            whole(wx_pairs.shape), whole((1, cb)), whole((1, cb))],
        out_specs=pl.BlockSpec((tb, cb), lambda t: (t, 0)),
        out_shape=jax.ShapeDtypeStruct((s, cb), BF16),
        scratch_shapes=[pltpu.VMEM((tb + LRU_PAD, cb), F32),
                        pltpu.VMEM((tb, cb), F32), pltpu.VMEM((tb, cb), F32),
                        pltpu.VMEM((tb, cb), F32), pltpu.VMEM((1, cb), F32)],
        compiler_params=_params(1),
        name="rg_lru",
    )(*([proj0] * (2 * LRU_COL_BLOCKS)), conv_w, row(conv_b), wa_pairs, row(ba),
      wx_pairs, row(bx), row(lam))


def _pair_block_diag(w):
    nb, bd, _ = w.shape
    wp = w.reshape(nb // 2, 2, bd, bd)
    z = jnp.zeros((nb // 2, bd, bd), w.dtype)
    top = jnp.concatenate([wp[:, 0], z], axis=2)
    bot = jnp.concatenate([z, wp[:, 1]], axis=2)
    return jnp.concatenate([top, bot], axis=1)


SB_LOG_F32_ZERO = -105.0


def _sb_logits(q, kb, mask):
    z = _dot_nt(q, kb) * (HEAD_DIM_C ** -0.5)
    sp = jnp.maximum(z, 0.0) + jnp.log(1.0 + jnp.exp2(jnp.abs(z) * -LOG2_E))
    if mask is not None:
        sp = jnp.where(mask, sp, 0.0)
    hi = sp.astype(BF16)
    lo = (sp - hi.astype(F32)).astype(BF16)
    return z, jnp.concatenate([hi, lo], axis=1)


def _sb_suffix_sums(hi_lo_list, tri2):
    rows = hi_lo_list[0].shape[0]
    out = _dot(jnp.concatenate(hi_lo_list, axis=0), tri2)
    return [out[t * rows:(t + 1) * rows] for t in range(len(hi_lo_list))]


def _sb_scores(q, kb, tri2, mask):
    z, hi_lo = _sb_logits(q, kb, mask)
    return z, _sb_suffix_sums([hi_lo], tri2)[0]


def _sb_weighted(z, incl, carry, vb, mask):
    w = jnp.exp(z - (incl + carry))
    if mask is not None:
        w = jnp.where(mask, w, 0.0)
    return _dot(w.astype(BF16), vb)


def _sb_body(q_ref, k_ref, v_ref, tri_prev_ref, tri_half_ref, o_ref):
    qi = pl.program_id(1)
    hd, nh = HEAD_DIM_C, SB_HALF
    heads = range(SB_HEADS_PER_STEP)
    tri_prev = tri_prev_ref[...]
    tri_half = tri_half_ref[...]
    row_id = lax.broadcasted_iota(jnp.int32, (nh, nh), 0)
    col_id = lax.broadcasted_iota(jnp.int32, (nh, nh), 1)
    before = col_id < row_id
    cols = lambda h: slice(h * hd, (h + 1) * hd)
    rows = lambda half: slice(half * nh, (half + 1) * nh)

    def kv(start, size, h):
        start = pl.multiple_of(start, nh)
        return k_ref[pl.ds(start, size), cols(h)], v_ref[pl.ds(start, size), cols(h)]

    @pl.when(qi == 0)
    def _first_block():
        for h in heads:
            k0, v0 = kv(0, nh, h)
            k1, v1 = kv(nh, nh, h)
            z, incl = _sb_scores(q_ref[rows(0), cols(h)], k0, tri_half, before)
            o_ref[rows(0), cols(h)] = _sb_weighted(z, incl, 0.0, v0, before).astype(o_ref.dtype)
            q_bot = q_ref[rows(1), cols(h)]
            zd, incl_d = _sb_scores(q_bot, k1, tri_half, before)
            zp, incl_p = _sb_scores(q_bot, k0, tri_half, None)
            acc = (_sb_weighted(zd, incl_d, 0.0, v1, before)
                   + _sb_weighted(zp, incl_p, incl_d[:, 0:1], v0, None))
            o_ref[rows(1), cols(h)] = acc.astype(o_ref.dtype)

    @pl.when(qi > 0)
    def _later_blocks():
        base = qi * SB_TQ
        tiles = [(h, half) for h in heads for half in range(SB_TQ // nh)]
        window_start = lambda half: base - SB_PREV + half * nh
        qs = [q_ref[rows(half), cols(h)] for h, half in tiles]
        kv_prev = [kv(window_start(half), SB_PREV, h) for h, half in tiles]
        kv_diag = [kv(window_start(half) + SB_PREV, nh, h) for h, half in tiles]
        l_prev = [_sb_logits(qs[t], kv_prev[t][0], None) for t in range(len(tiles))]
        l_diag = [_sb_logits(qs[t], kv_diag[t][0], before) for t in range(len(tiles))]
        i_prev = _sb_suffix_sums([hl for _, hl in l_prev], tri_prev)
        i_diag = _sb_suffix_sums([hl for _, hl in l_diag], tri_half)
        state = ()
        for t in range(len(tiles)):
            zd, incl_d, zp, incl_p = l_diag[t][0], i_diag[t], l_prev[t][0], i_prev[t]
            carry = incl_d[:, 0:1]
            acc = (_sb_weighted(zd, incl_d, 0.0, kv_diag[t][1], before)
                   + _sb_weighted(zp, incl_p, carry, kv_prev[t][1], None))
            state += (carry + incl_p[:, 0:1], acc)

        def live(loop_state):
            n, state = loop_state
            stick = jnp.min(functools.reduce(jnp.minimum, state[0::2]))
            more_keys = window_start(1) - (n + 1) * nh >= 0
            return jnp.logical_and(more_keys, -stick > SB_LOG_F32_ZERO)

        def body(loop_state):
            n, state = loop_state
            starts = [window_start(half) - (n + 1) * nh for _, half in tiles]
            kvs = [kv(jnp.maximum(starts[t], 0), nh, h) for t, (h, _) in enumerate(tiles)]
            logits = [_sb_logits(qs[t], kvs[t][0], None) for t in range(len(tiles))]
            incls = _sb_suffix_sums([hl for _, hl in logits], tri_half)
            out = ()
            for t in range(len(tiles)):
                valid = starts[t] >= 0
                carry, acc = state[2 * t], state[2 * t + 1]
                add = _sb_weighted(logits[t][0], incls[t], carry, kvs[t][1], None)
                out += (carry + jnp.where(valid, incls[t][:, 0:1], 0.0),
                        acc + jnp.where(valid, add, 0.0))
            return n + 1, out

        _, state = lax.while_loop(live, body, (jnp.int32(0), state))
        for t, (h, half) in enumerate(tiles):
            o_ref[rows(half), cols(h)] = state[2 * t + 1].astype(o_ref.dtype)


def _suffix_sum_matrix(n):
    jj = jnp.arange(n)
    tri = (jj[:, None] >= jj[None, :]).astype(BF16)
    return jnp.concatenate([tri, tri], axis=0)


def _stick_breaking(proj1):
    s = proj1.shape[0]
    hw = SB_HEADS_PER_STEP * HEAD_DIM_C
    n_groups = N_HEADS_C // SB_HEADS_PER_STEP
    whole = lambda shape: pl.BlockSpec(shape, lambda g, i: (0,) * len(shape))
    return pl.pallas_call(
        _sb_body,
        grid=(n_groups, s // SB_TQ),
        in_specs=[pl.BlockSpec((SB_TQ, hw), lambda g, i: (i, g)),
                  pl.BlockSpec((s, hw), lambda g, i: (0, n_groups + g)),
                  pl.BlockSpec((s, hw), lambda g, i: (0, 2 * n_groups + g)),
                  whole((2 * SB_PREV, SB_PREV)), whole((2 * SB_HALF, SB_HALF))],
        out_specs=pl.BlockSpec((SB_TQ, hw), lambda g, i: (i, g)),
        out_shape=jax.ShapeDtypeStruct((s, N_HEADS_C * HEAD_DIM_C), BF16),
        compiler_params=_params(2),
        name="stick_breaking",
    )(proj1, proj1, proj1, _suffix_sum_matrix(SB_PREV), _suffix_sum_matrix(SB_HALF))


def kernel(x, rel_bias, norm_mix, even_w_in, even_conv_w, even_conv_b, even_gate_a_w,
           even_gate_a_b, even_gate_x_w, even_gate_x_b, even_lru_lambda, even_sinks,
           even_w_out, odd_w_in, odd_w_out, norm_mlp, w_up, w_down, final_norm):
    b, s, d = x.shape
    h = x.reshape(b * s, d)

    proj0 = _norm_matmul(h, norm_mix[0], even_w_in[0], F32)
    y_a = _swa(proj0, rel_bias, even_sinks[0], _swa_bucket_table())
    y_b = _lru(proj0, even_conv_w[0], even_conv_b[0],
               _pair_block_diag(even_gate_a_w[0]).astype(BF16), even_gate_a_b[0],
               _pair_block_diag(even_gate_x_w[0]).astype(BF16), even_gate_x_b[0],
               even_lru_lambda[0])
    h = _matmul_res([y_a, y_b], even_w_out[0], h)
    h = _mlp(h, norm_mlp[0], w_up, w_down, 0, None)

    proj1 = _norm_matmul(h, norm_mix[1], odd_w_in[0], BF16)
    y = _stick_breaking(proj1)
    h = _matmul_res([y], odd_w_out[0], h)
    h = _mlp(h, norm_mlp[1], w_up, w_down, 1, final_norm)
    return h.reshape(b, s, d)
```

```python
import functools
import math

import jax
import jax.numpy as jnp
from jax import lax
from jax.experimental import pallas as pl
from jax.experimental.pallas import tpu as pltpu

F32 = jnp.float32
BF16 = jnp.bfloat16

CHUNK = 64
EPS = 1e-6

HEAD_DIM_A = 64
N_HEADS_A = 16
N_KV_A = 4
GROUP_A = N_HEADS_A // N_KV_A
WINDOW_CHUNKS = 2
Q_BLOCK = 128
HIST = WINDOW_CHUNKS * CHUNK
KEY_SPAN = Q_BLOCK + HIST
NUM_BUCKETS = 32
MAX_DISTANCE = 128

LRU_WIDTH = 1024
CONV_WIDTH = 4
LRU_C = 8.0

HEAD_DIM_C = 128
N_HEADS_C = 16

Q_COLS = N_HEADS_A * HEAD_DIM_A
KV_COLS = N_KV_A * HEAD_DIM_A
OFF_K = Q_COLS
OFF_V = OFF_K + KV_COLS
OFF_X = OFF_V + KV_COLS
OFF_G = OFF_X + LRU_WIDTH

V7X_LANES = 128
V7X_SUBLANES = 8
V7X_VMEM_LIMIT_BYTES = 56 * 1024 * 1024

MASK_NEG = -1e30
LOG2_E = math.log2(math.e)

PROJ_TM = 512
PROJ_TN_MAX = 1024
PROJ_LOAD_COLS = 256
PROJ_LOAD_SLOTS = 3
MLP_TM, MLP_TF = 1024, 512
SWA_BLOCKS_PER_STEP = 4
LRU_TIME_BLOCK = 512
LRU_COL_BLOCK = 512
SB_TQ = 256
SB_HALF = 128
SB_PREV = 256
SB_HEADS_PER_STEP = 4


def _params(n_grid_dims):
    return pltpu.CompilerParams(
        dimension_semantics=("arbitrary",) * n_grid_dims,
        vmem_limit_bytes=V7X_VMEM_LIMIT_BYTES)


def _proj_tn(n):
    tn = PROJ_TN_MAX
    while n % tn:
        tn //= 2
    return tn


def _rms_norm_f32(x, g):
    ms = jnp.mean(x * x, axis=-1, keepdims=True)
    return (x * lax.rsqrt(ms + EPS)) * g


def _dot(a, b):
    return jnp.dot(a, b, preferred_element_type=F32)


def _dot_nt(a, b):
    return lax.dot_general(a, b, (((1,), (1,)), ((), ())), preferred_element_type=F32)


def _resident(shape):
    return pl.BlockSpec(shape, lambda i: (0,) * len(shape), pipeline_mode=pl.Buffered(1))


def _weight_scratch(k, n):
    return [pltpu.VMEM((k, n), BF16),
            pltpu.VMEM((PROJ_LOAD_SLOTS, k, PROJ_LOAD_COLS), F32),
            pltpu.SemaphoreType.DMA((PROJ_LOAD_SLOTS,))]


def _for_each_column_chunk(w_hbm, w_ref, stage_ref, sem, tn, compute):
    n = w_ref.shape[1]
    lc, slots = PROJ_LOAD_COLS, PROJ_LOAD_SLOTS
    n_loads, loads_per_chunk = n // lc, tn // lc

    def copy(c):
        return pltpu.make_async_copy(w_hbm.at[:, pl.ds(c * lc, lc)], stage_ref.at[c % slots],
                                     sem.at[c % slots])

    @pl.when(pl.program_id(0) == 0)
    def _():
        for l in range(min(slots - 1, n_loads)):
            copy(l).start()
        for c in range(n // tn):
            for l in range(c * loads_per_chunk, (c + 1) * loads_per_chunk):
                if l + slots - 1 < n_loads:
                    copy(l + slots - 1).start()
                copy(l).wait()
                w_ref[:, l * lc:(l + 1) * lc] = stage_ref[l % slots].astype(BF16)
            compute(slice(c * tn, (c + 1) * tn))

    @pl.when(pl.program_id(0) > 0)
    def _():
        for c in range(n // tn):
            compute(slice(c * tn, (c + 1) * tn))


def _norm_matmul_body(x_ref, g_ref, w_hbm, o_ref, hn_ref, w_ref, stage_ref, sem):
    hn_ref[...] = _rms_norm_f32(x_ref[...], g_ref[...]).astype(BF16)

    def compute(cs):
        o_ref[:, cs] = _dot(hn_ref[...], w_ref[:, cs]).astype(o_ref.dtype)

    _for_each_column_chunk(w_hbm, w_ref, stage_ref, sem, _proj_tn(o_ref.shape[1]), compute)


def _norm_matmul(x, g, w, out_dtype):
    m, k = x.shape
    n = w.shape[1]
    tm = PROJ_TM
    return pl.pallas_call(
        _norm_matmul_body,
        grid=(m // tm,),
        in_specs=[pl.BlockSpec((tm, k), lambda i: (i, 0)),
                  _resident((1, k)), pl.BlockSpec(memory_space=pl.ANY)],
        out_specs=pl.BlockSpec((tm, n), lambda i: (i, 0)),
        out_shape=jax.ShapeDtypeStruct((m, n), out_dtype),
        scratch_shapes=[pltpu.VMEM((tm, k), BF16)] + _weight_scratch(k, n),
        compiler_params=_params(1),
        name="norm_matmul",
    )(x, g.reshape(1, k), w)


def _matmul_res_body(*refs, n_parts):
    a_refs = refs[:n_parts]
    w_hbm, res_ref, o_ref, w_ref, stage_ref, sem = refs[n_parts:]
    kp = w_ref.shape[0] // n_parts

    def compute(cs):
        acc = res_ref[:, cs]
        for p, a_ref in enumerate(a_refs):
            acc = acc + _dot(a_ref[...], w_ref[p * kp:(p + 1) * kp, cs])
        o_ref[:, cs] = acc

    _for_each_column_chunk(w_hbm, w_ref, stage_ref, sem, _proj_tn(o_ref.shape[1]), compute)


def _matmul_res(a_list, w, res):
    m, n = res.shape
    tm = PROJ_TM
    n_parts = len(a_list)
    kp = w.shape[0] // n_parts
    in_specs = [pl.BlockSpec((tm, kp), lambda i: (i, 0)) for _ in a_list]
    in_specs += [pl.BlockSpec(memory_space=pl.ANY), pl.BlockSpec((tm, n), lambda i: (i, 0))]
    return pl.pallas_call(
        functools.partial(_matmul_res_body, n_parts=n_parts),
        grid=(m // tm,),
        in_specs=in_specs,
        out_specs=pl.BlockSpec((tm, n), lambda i: (i, 0)),
        out_shape=jax.ShapeDtypeStruct((m, n), F32),
        scratch_shapes=_weight_scratch(w.shape[0], n),
        compiler_params=_params(1),
        name="matmul_res",
    )(*a_list, w, res)


def _mlp_body(*refs, final_norm):
    if final_norm:
        h_ref, g_ref, wu_ref, wd_ref, fg_ref, o_ref, hn_ref = refs
    else:
        h_ref, g_ref, wu_ref, wd_ref, o_ref, hn_ref = refs
    f = pl.program_id(1)

    @pl.when(f == 0)
    def _():
        h = h_ref[...]
        hn_ref[...] = _rms_norm_f32(h, g_ref[...]).astype(BF16)
        o_ref[...] = h

    up = _dot(hn_ref[...], wu_ref[...].astype(BF16))
    act = jnp.square(jnp.maximum(up, 0.0)).astype(BF16)
    o_ref[...] += _dot(act, wd_ref[...].astype(BF16))

    if final_norm:
        @pl.when(f == pl.num_programs(1) - 1)
        def _():
            o_ref[...] = _rms_norm_f32(o_ref[...], fg_ref[...])


def _mlp(h, g, w_up, w_down, layer, final_g):
    m, d = h.shape
    ff = w_up.shape[2]
    tm, tf = MLP_TM, MLP_TF
    final_norm = final_g is not None
    in_specs = [pl.BlockSpec((tm, d), lambda i, f: (i, 0)),
                pl.BlockSpec((1, d), lambda i, f: (0, 0)),
                pl.BlockSpec((None, d, tf), lambda i, f: (layer, 0, f)),
                pl.BlockSpec((None, tf, d), lambda i, f: (layer, f, 0))]
    args = [h, g.reshape(1, d), w_up, w_down]
    if final_norm:
        in_specs.append(pl.BlockSpec((1, d), lambda i, f: (0, 0)))
        args.append(final_g.reshape(1, d))
    return pl.pallas_call(
        functools.partial(_mlp_body, final_norm=final_norm),
        grid=(m // tm, ff // tf),
        in_specs=in_specs,
        out_specs=pl.BlockSpec((tm, d), lambda i, f: (i, 0)),
        out_shape=jax.ShapeDtypeStruct((m, d), F32),
        scratch_shapes=[pltpu.VMEM((tm, d), BF16)],
        compiler_params=_params(2),
        name="mlp",
    )(*args)


def _swa_body(relb_ref, sink_ref, bucket_ref, q_ref, kc_ref, kp_ref, vc_ref, vp_ref,
              o_ref, tab_ref):
    i = pl.program_id(0)
    nq = Q_BLOCK

    @pl.when(i == 0)
    def _():
        bucket = bucket_ref[...]
        for head in range(N_HEADS_A):
            t = jnp.full((KEY_SPAN, nq), MASK_NEG, F32)
            for b in range(NUM_BUCKETS):
                t = jnp.where(bucket == b, relb_ref[b, head], t)
            tab_ref[:, head * nq:(head + 1) * nq] = t

    k_cur, v_cur = kc_ref[...], vc_ref[...]
    first_pen = jnp.where(i == 0, MASK_NEG, 0.0).astype(F32)
    for b in range(SWA_BLOCKS_PER_STEP):
        own = slice(b * nq, (b + 1) * nq)
        if b == 0:
            k_hist, v_hist, pen = kp_ref[...], vp_ref[...], first_pen
        else:
            hist = slice((b - 1) * nq, b * nq)
            k_hist, v_hist, pen = k_cur[hist], v_cur[hist], None
        _swa_block(q_ref[own, :], k_hist, k_cur[own], v_hist, v_cur[own], pen,
                   tab_ref, sink_ref, o_ref.at[own, :])


def _swa_block(q, k_hist, k_own, v_hist, v_own, hist_pen, tab_ref, sink_ref, o_ref):
    hd, nq = HEAD_DIM_A, Q_BLOCK
    lanes = GROUP_A * nq
    q_t = (q * (hd ** -0.5)).T
    k_all = jnp.concatenate([k_hist, k_own], axis=0).astype(BF16)
    v_t = jnp.concatenate([v_hist.T, v_own.T], axis=1)
    zeros = jnp.zeros((hd, lanes), BF16)
    ones = jnp.ones((2 * V7X_SUBLANES, KEY_SPAN), BF16)

    kv_heads = range(N_KV_A)
    heads = [[h * GROUP_A + g for g in range(GROUP_A)] for h in kv_heads]
    rhs_rows = []
    for h in kv_heads:
        qg = jnp.concatenate([q_t[hd_ * hd:(hd_ + 1) * hd, :] for hd_ in heads[h]],
                             axis=1).astype(BF16)
        rhs_rows.append(jnp.concatenate([zeros] * h + [qg] + [zeros] * (N_KV_A - 1 - h), axis=1))
    logits = _dot(k_all, jnp.concatenate(rhs_rows, axis=0)) + tab_ref[...]
    if hist_pen is not None:
        logits = jnp.concatenate([logits[0:HIST] + hist_pen, logits[HIST:KEY_SPAN]], axis=0)
    sink = jnp.concatenate([jnp.full((1, nq), sink_ref[hd_], F32)
                            for h in kv_heads for hd_ in heads[h]], axis=1)
    m = jnp.maximum(jnp.max(logits, axis=0, keepdims=True), sink)
    p = jnp.exp(logits - m).astype(BF16)
    sink_p = jnp.exp(sink - m)
    for h in kv_heads:
        hs = slice(h * lanes, (h + 1) * lanes)
        v_aug = jnp.concatenate([v_t[h * hd:(h + 1) * hd, :].astype(BF16), ones], axis=0)
        acc = _dot(v_aug, p[:, hs])
        out_t = acc[0:hd, :] * (1.0 / (acc[hd:hd + 1, :] + sink_p[:, hs]))
        for g in range(0, GROUP_A, 2):
            both = jnp.concatenate([out_t[:, g * nq:(g + 1) * nq],
                                    out_t[:, (g + 1) * nq:(g + 2) * nq]], axis=0)
            c0 = heads[h][g] * hd
            o_ref[:, c0:c0 + 2 * hd] = both.T.astype(o_ref.dtype)


def _swa(proj0, rel_bias, sinks, bucket):
    s = proj0.shape[0]
    rows = SWA_BLOCKS_PER_STEP * Q_BLOCK
    k_blk, v_blk = OFF_K // KV_COLS, OFF_V // KV_COLS
    own = lambda blk: pl.BlockSpec((rows, KV_COLS), lambda i: (i, blk))
    before = lambda blk: pl.BlockSpec(
        (Q_BLOCK, KV_COLS), lambda i: (jnp.maximum(SWA_BLOCKS_PER_STEP * i - 1, 0), blk))
    return pl.pallas_call(
        _swa_body,
        grid=(s // rows,),
        in_specs=[pl.BlockSpec(memory_space=pltpu.SMEM),
                  pl.BlockSpec(memory_space=pltpu.SMEM),
                  pl.BlockSpec((KEY_SPAN, Q_BLOCK), lambda i: (0, 0)),
                  pl.BlockSpec((rows, Q_COLS), lambda i: (i, 0)),
                  own(k_blk), before(k_blk), own(v_blk), before(v_blk)],
        out_specs=pl.BlockSpec((rows, Q_COLS), lambda i: (i, 0)),
        out_shape=jax.ShapeDtypeStruct((s, Q_COLS), BF16),
        scratch_shapes=[pltpu.VMEM((KEY_SPAN, N_HEADS_A * Q_BLOCK), F32)],
        compiler_params=_params(1),
        name="swa_attention",
    )(rel_bias, sinks, bucket, proj0, proj0, proj0, proj0, proj0)


def _t5_bucket(rel):
    half = NUM_BUCKETS // 2
    ret = jnp.where(rel > 0, half, 0)
    n = jnp.abs(rel)
    max_exact = half // 2
    nf = jnp.maximum(n, 1).astype(F32)
    large = max_exact + (jnp.log(nf / max_exact) / math.log(MAX_DISTANCE / max_exact)
                         * (half - max_exact)).astype(jnp.int32)
    large = jnp.minimum(large, half - 1)
    return ret + jnp.where(n < max_exact, n, large)


def _swa_bucket_table():
    qi = jnp.arange(Q_BLOCK)
    sj = jnp.arange(KEY_SPAN)
    rel = (sj[:, None] - HIST) - qi[None, :]
    q_chunk = qi // CHUNK
    k_chunk = (sj - HIST) // CHUNK
    in_window = ((k_chunk[:, None] <= q_chunk[None, :])
                 & (k_chunk[:, None] >= q_chunk[None, :] - WINDOW_CHUNKS))
    return jnp.where(in_window, _t5_bucket(rel), -1).astype(jnp.int32)


LRU_PAD = V7X_SUBLANES
LRU_LANE_TILES = LRU_WIDTH // V7X_LANES
LRU_COL_BLOCKS = LRU_WIDTH // LRU_COL_BLOCK


def _neg_expm1(y, e):
    safe_log = jnp.where(e == 1.0, 1.0, jnp.log(e))
    near_zero = jnp.where(e == 1.0, -y, (1.0 - e) * y / safe_log)
    return jnp.where(y > -0.5, near_zero, 1.0 - e)


def _lru_body(*refs):
    xb_refs = refs[:LRU_COL_BLOCKS]
    gb_refs = refs[LRU_COL_BLOCKS:2 * LRU_COL_BLOCKS]
    (cw_ref, cb_ref, wa_ref, ba_ref, wx_ref, bx_ref, lam_ref,
     o_ref, xpad_ref, a_ref, u_ref, hs_ref, hc_ref) = refs[2 * LRU_COL_BLOCKS:]
    t = pl.program_id(0)
    tb, cb = LRU_TIME_BLOCK, LRU_WIDTH

    @pl.when(t == 0)
    def _():
        xpad_ref[0:LRU_PAD, :] = jnp.zeros((LRU_PAD, cb), F32)
        hc_ref[...] = jnp.zeros((1, cb), F32)

    x = jnp.concatenate([r[...] for r in xb_refs], axis=1)
    gb = jnp.concatenate([r[...] for r in gb_refs], axis=1)
    xpad_ref[LRU_PAD:LRU_PAD + tb, :] = x
    xc = cb_ref[...] + cw_ref[CONV_WIDTH - 1:CONV_WIDTH, :] * x
    for w in range(CONV_WIDTH - 1):
        off = LRU_PAD - (CONV_WIDTH - 1) + w
        xc = xc + cw_ref[w:w + 1, :] * xpad_ref[off:off + tb, :]
    xpad_ref[0:LRU_PAD, :] = x[tb - LRU_PAD:tb, :]

    neg_c_softplus = -LRU_C * jax.nn.softplus(-lam_ref[...])
    for c in range(LRU_LANE_TILES):
        cs = slice(c * V7X_LANES, (c + 1) * V7X_LANES)
        xcc = xc[:, cs]
        xcb = xcc.astype(BF16)
        r = jax.nn.sigmoid(_dot(xcb, wa_ref[c]) + ba_ref[:, cs])
        ig = jax.nn.sigmoid(_dot(xcb, wx_ref[c]) + bx_ref[:, cs])
        log_a = r * neg_c_softplus[:, cs]
        a = jnp.exp(log_a)
        a_ref[:, cs] = a
        u_ref[:, cs] = jnp.sqrt(_neg_expm1(2.0 * log_a, a * a)) * (ig * xcc)

    def scan_rows(r8, h):
        base = pl.multiple_of(r8 * V7X_SUBLANES, V7X_SUBLANES)
        for r in range(V7X_SUBLANES):
            h = a_ref[pl.ds(base + r, 1), :] * h + u_ref[pl.ds(base + r, 1), :]
            hs_ref[pl.ds(base + r, 1), :] = h
        return h

    hc_ref[...] = lax.fori_loop(0, tb // V7X_SUBLANES, scan_rows, hc_ref[...])
    o_ref[...] = (jax.nn.gelu(gb, approximate=True) * hs_ref[...]).astype(o_ref.dtype)


def _lru(proj0, conv_w, conv_b, wa_pairs, ba, wx_pairs, bx, lam):
    s = proj0.shape[0]
    tb, cb, wb = LRU_TIME_BLOCK, LRU_WIDTH, LRU_COL_BLOCK
    row = lambda v: v.reshape(1, cb)
    whole = lambda shape: pl.BlockSpec(shape, lambda t: (0,) * len(shape))
    col_blocks = lambda off: [
        pl.BlockSpec((tb, wb), functools.partial(lambda t, c: (t, c), c=off // wb + c))
        for c in range(LRU_COL_BLOCKS)]
    return pl.pallas_call(
        _lru_body,
        grid=(s // tb,),
        in_specs=col_blocks(OFF_X) + col_blocks(OFF_G) + [
            whole((CONV_WIDTH, cb)), whole((1, cb)),
            whole(wa_pairs.shape), whole((1, cb)),
            whole(wx_pairs.shape), whole((1, cb)), whole((1, cb))],
        out_specs=pl.BlockSpec((tb, cb), lambda t: (t, 0)),
        out_shape=jax.ShapeDtypeStruct((s, cb), BF16),
        scratch_shapes=[pltpu.VMEM((tb + LRU_PAD, cb), F32),
                        pltpu.VMEM((tb, cb), F32), pltpu.VMEM((tb, cb), F32),
                        pltpu.VMEM((tb, cb), F32), pltpu.VMEM((1, cb), F32)],
        compiler_params=_params(1),
        name="rg_lru",
    )(*([proj0] * (2 * LRU_COL_BLOCKS)), conv_w, row(conv_b), wa_pairs, row(ba),
      wx_pairs, row(bx), row(lam))


def _pair_block_diag(w):
    nb, bd, _ = w.shape
    wp = w.reshape(nb // 2, 2, bd, bd)
    z = jnp.zeros((nb // 2, bd, bd), w.dtype)
    top = jnp.concatenate([wp[:, 0], z], axis=2)
    bot = jnp.concatenate([z, wp[:, 1]], axis=2)
    return jnp.concatenate([top, bot], axis=1)


SB_LOG_F32_ZERO = -105.0


def _sb_logits(q, kb, mask):
    z = _dot_nt(q, kb) * (HEAD_DIM_C ** -0.5)
    sp = jnp.maximum(z, 0.0) + jnp.log(1.0 + jnp.exp2(jnp.abs(z) * -LOG2_E))
    if mask is not None:
        sp = jnp.where(mask, sp, 0.0)
    return z, sp.astype(BF16)


def _sb_suffix_sums(hi_lo_list, tri2):
    rows = hi_lo_list[0].shape[0]
    out = _dot(jnp.concatenate(hi_lo_list, axis=0), tri2)
    return [out[t * rows:(t + 1) * rows] for t in range(len(hi_lo_list))]


def _sb_scores(q, kb, tri2, mask):
    z, hi_lo = _sb_logits(q, kb, mask)
    return z, _sb_suffix_sums([hi_lo], tri2)[0]


def _sb_weighted(z, incl, carry, vb, mask):
    w = jnp.exp(z - (incl + carry))
    if mask is not None:
        w = jnp.where(mask, w, 0.0)
    return _dot(w.astype(BF16), vb)


def _sb_body(q_ref, k_ref, v_ref, tri_prev_ref, tri_half_ref, o_ref):
    qi = pl.program_id(1)
    hd, nh = HEAD_DIM_C, SB_HALF
    heads = range(SB_HEADS_PER_STEP)
    tri_prev = tri_prev_ref[...]
    tri_half = tri_half_ref[...]
    row_id = lax.broadcasted_iota(jnp.int32, (nh, nh), 0)
    col_id = lax.broadcasted_iota(jnp.int32, (nh, nh), 1)
    before = col_id < row_id
    cols = lambda h: slice(h * hd, (h + 1) * hd)
    rows = lambda half: slice(half * nh, (half + 1) * nh)

    def kv(start, size, h):
        start = pl.multiple_of(start, nh)
        return k_ref[pl.ds(start, size), cols(h)], v_ref[pl.ds(start, size), cols(h)]

    @pl.when(qi == 0)
    def _first_block():
        for h in heads:
            k0, v0 = kv(0, nh, h)
            k1, v1 = kv(nh, nh, h)
            z, incl = _sb_scores(q_ref[rows(0), cols(h)], k0, tri_half, before)
            o_ref[rows(0), cols(h)] = _sb_weighted(z, incl, 0.0, v0, before).astype(o_ref.dtype)
            q_bot = q_ref[rows(1), cols(h)]
            zd, incl_d = _sb_scores(q_bot, k1, tri_half, before)
            zp, incl_p = _sb_scores(q_bot, k0, tri_half, None)
            acc = (_sb_weighted(zd, incl_d, 0.0, v1, before)
                   + _sb_weighted(zp, incl_p, incl_d[:, 0:1], v0, None))
            o_ref[rows(1), cols(h)] = acc.astype(o_ref.dtype)

    @pl.when(qi > 0)
    def _later_blocks():
        base = qi * SB_TQ
        tiles = [(h, half) for h in heads for half in range(SB_TQ // nh)]
        window_start = lambda half: base - SB_PREV + half * nh
        qs = [q_ref[rows(half), cols(h)] for h, half in tiles]
        kv_prev = [kv(window_start(half), SB_PREV, h) for h, half in tiles]
        kv_diag = [kv(window_start(half) + SB_PREV, nh, h) for h, half in tiles]
        l_diag = [_sb_logits(qs[t], kv_diag[t][0], before) for t in range(len(tiles))]
        l_prev = [_sb_logits(qs[t], kv_prev[t][0], None) for t in range(len(tiles))]
        i_diag = _sb_suffix_sums([hl for _, hl in l_diag], tri_half)
        i_prev = _sb_suffix_sums([hl for _, hl in l_prev], tri_prev)
        state = ()
        for t in range(len(tiles)):
            zd, incl_d, zp, incl_p = l_diag[t][0], i_diag[t], l_prev[t][0], i_prev[t]
            carry = incl_d[:, 0:1]
            acc = (_sb_weighted(zd, incl_d, 0.0, kv_diag[t][1], before)
                   + _sb_weighted(zp, incl_p, carry, kv_prev[t][1], None))
            state += (carry + incl_p[:, 0:1], acc)

        def live(loop_state):
            n, state = loop_state
            stick = jnp.min(functools.reduce(jnp.minimum, state[0::2]))
            more_keys = window_start(1) - (n + 1) * nh >= 0
            return jnp.logical_and(more_keys, -stick > SB_LOG_F32_ZERO)

        def body(loop_state):
            n, state = loop_state
            starts = [window_start(half) - (n + 1) * nh for _, half in tiles]
            kvs = [kv(jnp.maximum(starts[t], 0), nh, h) for t, (h, _) in enumerate(tiles)]
            logits = [_sb_logits(qs[t], kvs[t][0], None) for t in range(len(tiles))]
            incls = _sb_suffix_sums([hl for _, hl in logits], tri_half)
            out = ()
            for t in range(len(tiles)):
                valid = starts[t] >= 0
                carry, acc = state[2 * t], state[2 * t + 1]
                add = _sb_weighted(logits[t][0], incls[t], carry, kvs[t][1], None)
                out += (carry + jnp.where(valid, incls[t][:, 0:1], 0.0),
                        acc + jnp.where(valid, add, 0.0))
            return n + 1, out

        _, state = lax.while_loop(live, body, (jnp.int32(0), state))
        for t, (h, half) in enumerate(tiles):
            o_ref[rows(half), cols(h)] = state[2 * t + 1].astype(o_ref.dtype)


def _suffix_sum_matrix(n):
    jj = jnp.arange(n)
    return (jj[:, None] >= jj[None, :]).astype(BF16)


def _stick_breaking(proj1):
    s = proj1.shape[0]
    hw = SB_HEADS_PER_STEP * HEAD_DIM_C
    n_groups = N_HEADS_C // SB_HEADS_PER_STEP
    whole = lambda shape: pl.BlockSpec(shape, lambda g, i: (0,) * len(shape))
    return pl.pallas_call(
        _sb_body,
        grid=(n_groups, s // SB_TQ),
        in_specs=[pl.BlockSpec((SB_TQ, hw), lambda g, i: (i, g)),
                  pl.BlockSpec((s, hw), lambda g, i: (0, n_groups + g)),
                  pl.BlockSpec((s, hw), lambda g, i: (0, 2 * n_groups + g)),
                  whole((SB_PREV, SB_PREV)), whole((SB_HALF, SB_HALF))],
        out_specs=pl.BlockSpec((SB_TQ, hw), lambda g, i: (i, g)),
        out_shape=jax.ShapeDtypeStruct((s, N_HEADS_C * HEAD_DIM_C), BF16),
        compiler_params=_params(2),
        name="stick_breaking",
    )(proj1, proj1, proj1, _suffix_sum_matrix(SB_PREV), _suffix_sum_matrix(SB_HALF))


def kernel(x, rel_bias, norm_mix, even_w_in, even_conv_w, even_conv_b, even_gate_a_w,
           even_gate_a_b, even_gate_x_w, even_gate_x_b, even_lru_lambda, even_sinks,
           even_w_out, odd_w_in, odd_w_out, norm_mlp, w_up, w_down, final_norm):
    b, s, d = x.shape
    h = x.reshape(b * s, d)

    proj0 = _norm_matmul(h, norm_mix[0], even_w_in[0], F32)
    y_a = _swa(proj0, rel_bias, even_sinks[0], _swa_bucket_table())
    y_b = _lru(proj0, even_conv_w[0], even_conv_b[0],
               _pair_block_diag(even_gate_a_w[0]).astype(BF16), even_gate_a_b[0],
               _pair_block_diag(even_gate_x_w[0]).astype(BF16), even_gate_x_b[0],
               even_lru_lambda[0])
    h = _matmul_res([y_a, y_b], even_w_out[0], h)
    h = _mlp(h, norm_mlp[0], w_up, w_down, 0, None)

    proj1 = _norm_matmul(h, norm_mix[1], odd_w_in[0], BF16)
    y = _stick_breaking(proj1)
    h = _matmul_res([y], odd_w_out[0], h)
    h = _mlp(h, norm_mlp[1], w_up, w_down, 1, final_norm)
    return h.reshape(b, s, d)
```

```python
import functools
import math

import jax
import jax.numpy as jnp
from jax import lax
from jax.experimental import pallas as pl
from jax.experimental.pallas import tpu as pltpu

F32 = jnp.float32
BF16 = jnp.bfloat16

CHUNK = 64
EPS = 1e-6

HEAD_DIM_A = 64
N_HEADS_A = 16
N_KV_A = 4
GROUP_A = N_HEADS_A // N_KV_A
WINDOW_CHUNKS = 2
Q_BLOCK = 128
HIST = WINDOW_CHUNKS * CHUNK
KEY_SPAN = Q_BLOCK + HIST
NUM_BUCKETS = 32
MAX_DISTANCE = 128

LRU_WIDTH = 1024
CONV_WIDTH = 4
LRU_C = 8.0

HEAD_DIM_C = 128
N_HEADS_C = 16

Q_COLS = N_HEADS_A * HEAD_DIM_A
KV_COLS = N_KV_A * HEAD_DIM_A
OFF_K = Q_COLS
OFF_V = OFF_K + KV_COLS
OFF_X = OFF_V + KV_COLS
OFF_G = OFF_X + LRU_WIDTH

V7X_LANES = 128
V7X_SUBLANES = 8
V7X_VMEM_LIMIT_BYTES = 56 * 1024 * 1024

MASK_NEG = -1e30
LOG2_E = math.log2(math.e)

PROJ_TM = 512
PROJ_TN_MAX = 1024
PROJ_LOAD_COLS = 256
PROJ_LOAD_SLOTS = 3
MLP_TM, MLP_TF = 1024, 512
SWA_BLOCKS_PER_STEP = 4
LRU_TIME_BLOCK = 512
LRU_COL_BLOCK = 512
SB_TQ = 256
SB_HALF = 128
SB_PREV = 256
SB_HEADS_PER_STEP = 4


def _params(n_grid_dims):
    return pltpu.CompilerParams(
        dimension_semantics=("arbitrary",) * n_grid_dims,
        vmem_limit_bytes=V7X_VMEM_LIMIT_BYTES)


def _proj_tn(n):
    tn = PROJ_TN_MAX
    while n % tn:
        tn //= 2
    return tn


def _rms_norm_f32(x, g):
    ms = jnp.mean(x * x, axis=-1, keepdims=True)
    return (x * lax.rsqrt(ms + EPS)) * g


def _dot(a, b):
    return jnp.dot(a, b, preferred_element_type=F32)


def _dot_nt(a, b):
    return lax.dot_general(a, b, (((1,), (1,)), ((), ())), preferred_element_type=F32)


def _resident(shape):
    return pl.BlockSpec(shape, lambda i: (0,) * len(shape), pipeline_mode=pl.Buffered(1))


def _weight_scratch(k, n):
    return [pltpu.VMEM((k, n), BF16),
            pltpu.VMEM((PROJ_LOAD_SLOTS, k, PROJ_LOAD_COLS), F32),
            pltpu.SemaphoreType.DMA((PROJ_LOAD_SLOTS,))]


def _for_each_column_chunk(w_hbm, w_ref, stage_ref, sem, tn, compute):
    n = w_ref.shape[1]
    lc, slots = PROJ_LOAD_COLS, PROJ_LOAD_SLOTS
    n_loads, loads_per_chunk = n // lc, tn // lc

    def copy(c):
        return pltpu.make_async_copy(w_hbm.at[:, pl.ds(c * lc, lc)], stage_ref.at[c % slots],
                                     sem.at[c % slots])

    @pl.when(pl.program_id(0) == 0)
    def _():
        for l in range(min(slots - 1, n_loads)):
            copy(l).start()
        for c in range(n // tn):
            for l in range(c * loads_per_chunk, (c + 1) * loads_per_chunk):
                if l + slots - 1 < n_loads:
                    copy(l + slots - 1).start()
                copy(l).wait()
                w_ref[:, l * lc:(l + 1) * lc] = stage_ref[l % slots].astype(BF16)
            compute(slice(c * tn, (c + 1) * tn))

    @pl.when(pl.program_id(0) > 0)
    def _():
        for c in range(n // tn):
            compute(slice(c * tn, (c + 1) * tn))


def _norm_matmul_body(x_ref, g_ref, w_hbm, o_ref, hn_ref, w_ref, stage_ref, sem):
    hn_ref[...] = _rms_norm_f32(x_ref[...], g_ref[...]).astype(BF16)

    def compute(cs):
        o_ref[:, cs] = _dot(hn_ref[...], w_ref[:, cs]).astype(o_ref.dtype)

    _for_each_column_chunk(w_hbm, w_ref, stage_ref, sem, _proj_tn(o_ref.shape[1]), compute)


def _norm_matmul(x, g, w, out_dtype):
    m, k = x.shape
    n = w.shape[1]
    tm = PROJ_TM
    return pl.pallas_call(
        _norm_matmul_body,
        grid=(m // tm,),
        in_specs=[pl.BlockSpec((tm, k), lambda i: (i, 0)),
                  _resident((1, k)), pl.BlockSpec(memory_space=pl.ANY)],
        out_specs=pl.BlockSpec((tm, n), lambda i: (i, 0)),
        out_shape=jax.ShapeDtypeStruct((m, n), out_dtype),
        scratch_shapes=[pltpu.VMEM((tm, k), BF16)] + _weight_scratch(k, n),
        compiler_params=_params(1),
        name="norm_matmul",
    )(x, g.reshape(1, k), w)


def _matmul_res_body(*refs, n_parts):
    a_refs = refs[:n_parts]
    w_hbm, res_ref, o_ref, w_ref, stage_ref, sem = refs[n_parts:]
    kp = w_ref.shape[0] // n_parts

    def compute(cs):
        acc = res_ref[:, cs]
        for p, a_ref in enumerate(a_refs):
            acc = acc + _dot(a_ref[...], w_ref[p * kp:(p + 1) * kp, cs])
        o_ref[:, cs] = acc

    _for_each_column_chunk(w_hbm, w_ref, stage_ref, sem, _proj_tn(o_ref.shape[1]), compute)


def _matmul_res(a_list, w, res):
    m, n = res.shape
    tm = PROJ_TM
    n_parts = len(a_list)
    kp = w.shape[0] // n_parts
    in_specs = [pl.BlockSpec((tm, kp), lambda i: (i, 0)) for _ in a_list]
    in_specs += [pl.BlockSpec(memory_space=pl.ANY), pl.BlockSpec((tm, n), lambda i: (i, 0))]
    return pl.pallas_call(
        functools.partial(_matmul_res_body, n_parts=n_parts),
        grid=(m // tm,),
        in_specs=in_specs,
        out_specs=pl.BlockSpec((tm, n), lambda i: (i, 0)),
        out_shape=jax.ShapeDtypeStruct((m, n), F32),
        scratch_shapes=_weight_scratch(w.shape[0], n),
        compiler_params=_params(1),
        name="matmul_res",
    )(*a_list, w, res)


def _mlp_body(*refs, final_norm):
    if final_norm:
        h_ref, g_ref, wu_ref, wd_ref, fg_ref, o_ref, hn_ref = refs
    else:
        h_ref, g_ref, wu_ref, wd_ref, o_ref, hn_ref = refs
    f = pl.program_id(1)

    @pl.when(f == 0)
    def _():
        h = h_ref[...]
        hn_ref[...] = _rms_norm_f32(h, g_ref[...]).astype(BF16)
        o_ref[...] = h

    up = _dot(hn_ref[...], wu_ref[...].astype(BF16))
    act = jnp.square(jnp.maximum(up, 0.0)).astype(BF16)
    o_ref[...] += _dot(act, wd_ref[...].astype(BF16))

    if final_norm:
        @pl.when(f == pl.num_programs(1) - 1)
        def _():
            o_ref[...] = _rms_norm_f32(o_ref[...], fg_ref[...])


def _mlp(h, g, w_up, w_down, layer, final_g):
    m, d = h.shape
    ff = w_up.shape[2]
    tm, tf = MLP_TM, MLP_TF
    final_norm = final_g is not None
    in_specs = [pl.BlockSpec((tm, d), lambda i, f: (i, 0)),
                pl.BlockSpec((1, d), lambda i, f: (0, 0)),
                pl.BlockSpec((None, d, tf), lambda i, f: (layer, 0, f)),
                pl.BlockSpec((None, tf, d), lambda i, f: (layer, f, 0))]
    args = [h, g.reshape(1, d), w_up, w_down]
    if final_norm:
        in_specs.append(pl.BlockSpec((1, d), lambda i, f: (0, 0)))
        args.append(final_g.reshape(1, d))
    return pl.pallas_call(
        functools.partial(_mlp_body, final_norm=final_norm),
        grid=(m // tm, ff // tf),
        in_specs=in_specs,
        out_specs=pl.BlockSpec((tm, d), lambda i, f: (i, 0)),
        out_shape=jax.ShapeDtypeStruct((m, d), F32),
        scratch_shapes=[pltpu.VMEM((tm, d), BF16)],
        compiler_params=_params(2),
        name="mlp",
    )(*args)


def _swa_body(relb_ref, sink_ref, bucket_ref, q_ref, kc_ref, kp_ref, vc_ref, vp_ref,
              o_ref, tab_ref):
    i = pl.program_id(0)
    nq = Q_BLOCK

    @pl.when(i == 0)
    def _():
        bucket = bucket_ref[...]
        for head in range(N_HEADS_A):
            t = jnp.full((KEY_SPAN, nq), MASK_NEG, F32)
            for b in range(NUM_BUCKETS):
                t = jnp.where(bucket == b, relb_ref[b, head], t)
            tab_ref[:, head * nq:(head + 1) * nq] = t

    k_cur, v_cur = kc_ref[...], vc_ref[...]
    first_pen = jnp.where(i == 0, MASK_NEG, 0.0).astype(F32)
    for b in range(SWA_BLOCKS_PER_STEP):
        own = slice(b * nq, (b + 1) * nq)
        if b == 0:
            k_hist, v_hist, pen = kp_ref[...], vp_ref[...], first_pen
        else:
            hist = slice((b - 1) * nq, b * nq)
            k_hist, v_hist, pen = k_cur[hist], v_cur[hist], None
        _swa_block(q_ref[own, :], k_hist, k_cur[own], v_hist, v_cur[own], pen,
                   tab_ref, sink_ref, o_ref.at[own, :])


def _swa_block(q, k_hist, k_own, v_hist, v_own, hist_pen, tab_ref, sink_ref, o_ref):
    hd, nq = HEAD_DIM_A, Q_BLOCK
    lanes = GROUP_A * nq
    q_t = (q * (hd ** -0.5)).T
    k_all = jnp.concatenate([k_hist, k_own], axis=0).astype(BF16)
    v_t = jnp.concatenate([v_hist.T, v_own.T], axis=1)
    zeros = jnp.zeros((hd, lanes), BF16)
    ones = jnp.ones((2 * V7X_SUBLANES, KEY_SPAN), BF16)

    kv_heads = range(N_KV_A)
    heads = [[h * GROUP_A + g for g in range(GROUP_A)] for h in kv_heads]
    rhs_rows = []
    for h in kv_heads:
        qg = jnp.concatenate([q_t[hd_ * hd:(hd_ + 1) * hd, :] for hd_ in heads[h]],
                             axis=1).astype(BF16)
        rhs_rows.append(jnp.concatenate([zeros] * h + [qg] + [zeros] * (N_KV_A - 1 - h), axis=1))
    logits = _dot(k_all, jnp.concatenate(rhs_rows, axis=0)) + tab_ref[...]
    if hist_pen is not None:
        logits = jnp.concatenate([logits[0:HIST] + hist_pen, logits[HIST:KEY_SPAN]], axis=0)
    sink = jnp.concatenate([jnp.full((1, nq), sink_ref[hd_], F32)
                            for h in kv_heads for hd_ in heads[h]], axis=1)
    m = jnp.maximum(jnp.max(logits, axis=0, keepdims=True), sink)
    p = jnp.exp(logits - m).astype(BF16)
    sink_p = jnp.exp(sink - m)
    for h in kv_heads:
        hs = slice(h * lanes, (h + 1) * lanes)
        v_aug = jnp.concatenate([v_t[h * hd:(h + 1) * hd, :].astype(BF16), ones], axis=0)
        acc = _dot(v_aug, p[:, hs])
        out_t = acc[0:hd, :] * (1.0 / (acc[hd:hd + 1, :] + sink_p[:, hs]))
        for g in range(0, GROUP_A, 2):
            both = jnp.concatenate([out_t[:, g * nq:(g + 1) * nq],
                                    out_t[:, (g + 1) * nq:(g + 2) * nq]], axis=0)
            c0 = heads[h][g] * hd
            o_ref[:, c0:c0 + 2 * hd] = both.T.astype(o_ref.dtype)


def _swa(proj0, rel_bias, sinks, bucket):
    s = proj0.shape[0]
    rows = SWA_BLOCKS_PER_STEP * Q_BLOCK
    k_blk, v_blk = OFF_K // KV_COLS, OFF_V // KV_COLS
    own = lambda blk: pl.BlockSpec((rows, KV_COLS), lambda i: (i, blk))
    before = lambda blk: pl.BlockSpec(
        (Q_BLOCK, KV_COLS), lambda i: (jnp.maximum(SWA_BLOCKS_PER_STEP * i - 1, 0), blk))
    return pl.pallas_call(
        _swa_body,
        grid=(s // rows,),
        in_specs=[pl.BlockSpec(memory_space=pltpu.SMEM),
                  pl.BlockSpec(memory_space=pltpu.SMEM),
                  pl.BlockSpec((KEY_SPAN, Q_BLOCK), lambda i: (0, 0)),
                  pl.BlockSpec((rows, Q_COLS), lambda i: (i, 0)),
                  own(k_blk), before(k_blk), own(v_blk), before(v_blk)],
        out_specs=pl.BlockSpec((rows, Q_COLS), lambda i: (i, 0)),
        out_shape=jax.ShapeDtypeStruct((s, Q_COLS), BF16),
        scratch_shapes=[pltpu.VMEM((KEY_SPAN, N_HEADS_A * Q_BLOCK), F32)],
        compiler_params=_params(1),
        name="swa_attention",
    )(rel_bias, sinks, bucket, proj0, proj0, proj0, proj0, proj0)


def _t5_bucket(rel):
    half = NUM_BUCKETS // 2
    ret = jnp.where(rel > 0, half, 0)
    n = jnp.abs(rel)
    max_exact = half // 2
    nf = jnp.maximum(n, 1).astype(F32)
    large = max_exact + (jnp.log(nf / max_exact) / math.log(MAX_DISTANCE / max_exact)
                         * (half - max_exact)).astype(jnp.int32)
    large = jnp.minimum(large, half - 1)
    return ret + jnp.where(n < max_exact, n, large)


def _swa_bucket_table():
    qi = jnp.arange(Q_BLOCK)
    sj = jnp.arange(KEY_SPAN)
    rel = (sj[:, None] - HIST) - qi[None, :]
    q_chunk = qi // CHUNK
    k_chunk = (sj - HIST) // CHUNK
    in_window = ((k_chunk[:, None] <= q_chunk[None, :])
                 & (k_chunk[:, None] >= q_chunk[None, :] - WINDOW_CHUNKS))
    return jnp.where(in_window, _t5_bucket(rel), -1).astype(jnp.int32)


LRU_PAD = V7X_SUBLANES
LRU_LANE_TILES = LRU_WIDTH // V7X_LANES
LRU_COL_BLOCKS = LRU_WIDTH // LRU_COL_BLOCK


def _neg_expm1(y, e):
    safe_log = jnp.where(e == 1.0, 1.0, jnp.log(e))
    near_zero = jnp.where(e == 1.0, -y, (1.0 - e) * y / safe_log)
    return jnp.where(y > -0.5, near_zero, 1.0 - e)


def _lru_body(*refs):
    xb_refs = refs[:LRU_COL_BLOCKS]
    gb_refs = refs[LRU_COL_BLOCKS:2 * LRU_COL_BLOCKS]
    (cw_ref, cb_ref, wa_ref, ba_ref, wx_ref, bx_ref, lam_ref,
     o_ref, xpad_ref, a_ref, u_ref, hs_ref, hc_ref) = refs[2 * LRU_COL_BLOCKS:]
    t = pl.program_id(0)
    tb, cb = LRU_TIME_BLOCK, LRU_WIDTH

    @pl.when(t == 0)
    def _():
        xpad_ref[0:LRU_PAD, :] = jnp.zeros((LRU_PAD, cb), F32)
        hc_ref[...] = jnp.zeros((1, cb), F32)

    x = jnp.concatenate([r[...] for r in xb_refs], axis=1)
    gb = jnp.concatenate([r[...] for r in gb_refs], axis=1)
    xpad_ref[LRU_PAD:LRU_PAD + tb, :] = x
    xc = cb_ref[...] + cw_ref[CONV_WIDTH - 1:CONV_WIDTH, :] * x
    for w in range(CONV_WIDTH - 1):
        off = LRU_PAD - (CONV_WIDTH - 1) + w
        xc = xc + cw_ref[w:w + 1, :] * xpad_ref[off:off + tb, :]
    xpad_ref[0:LRU_PAD, :] = x[tb - LRU_PAD:tb, :]

    neg_c_softplus = -LRU_C * jax.nn.softplus(-lam_ref[...])
    for c in range(LRU_LANE_TILES):
        cs = slice(c * V7X_LANES, (c + 1) * V7X_LANES)
        xcc = xc[:, cs]
        xcb = xcc.astype(BF16)
        r = jax.nn.sigmoid(_dot(xcb, wa_ref[c]) + ba_ref[:, cs])
        ig = jax.nn.sigmoid(_dot(xcb, wx_ref[c]) + bx_ref[:, cs])
        log_a = r * neg_c_softplus[:, cs]
        a = jnp.exp(log_a)
        a_ref[:, cs] = a
        u_ref[:, cs] = jnp.sqrt(_neg_expm1(2.0 * log_a, a * a)) * (ig * xcc)

    def scan_rows(r8, h):
        base = pl.multiple_of(r8 * V7X_SUBLANES, V7X_SUBLANES)
        for r in range(V7X_SUBLANES):
            h = a_ref[pl.ds(base + r, 1), :] * h + u_ref[pl.ds(base + r, 1), :]
            hs_ref[pl.ds(base + r, 1), :] = h
        return h

    hc_ref[...] = lax.fori_loop(0, tb // V7X_SUBLANES, scan_rows, hc_ref[...])
    o_ref[...] = (jax.nn.gelu(gb, approximate=True) * hs_ref[...]).astype(o_ref.dtype)


def _lru(proj0, conv_w, conv_b, wa_pairs, ba, wx_pairs, bx, lam):
    s = proj0.shape[0]
    tb, cb, wb = LRU_TIME_BLOCK, LRU_WIDTH, LRU_COL_BLOCK
    row = lambda v: v.reshape(1, cb)
    whole = lambda shape: pl.BlockSpec(shape, lambda t: (0,) * len(shape))
    col_blocks = lambda off: [
        pl.BlockSpec((tb, wb), functools.partial(lambda t, c: (t, c), c=off // wb + c))
        for c in range(LRU_COL_BLOCKS)]
    return pl.pallas_call(
        _lru_body,
        grid=(s // tb,),
        in_specs=col_blocks(OFF_X) + col_blocks(OFF_G) + [
            whole((CONV_WIDTH, cb)), whole((1, cb)),
            whole(wa_pairs.shape), whole((1, cb)),
            whole(wx_pairs.shape), whole((1, cb)), whole((1, cb))],
        out_specs=pl.BlockSpec((tb, cb), lambda t: (t, 0)),
        out_shape=jax.ShapeDtypeStruct((s, cb), BF16),
        scratch_shapes=[pltpu.VMEM((tb + LRU_PAD, cb), F32),
                        pltpu.VMEM((tb, cb), F32), pltpu.VMEM((tb, cb), F32),
                        pltpu.VMEM((tb, cb), F32), pltpu.VMEM((1, cb), F32)],
        compiler_params=_params(1),
        name="rg_lru",
    )(*([proj0] * (2 * LRU_COL_BLOCKS)), conv_w, row(conv_b), wa_pairs, row(ba),
      wx_pairs, row(bx), row(lam))


def _pair_block_diag(w):
    nb, bd, _ = w.shape
    wp = w.reshape(nb // 2, 2, bd, bd)
    z = jnp.zeros((nb // 2, bd, bd), w.dtype)
    top = jnp.concatenate([wp[:, 0], z], axis=2)
    bot = jnp.concatenate([z, wp[:, 1]], axis=2)
    return jnp.concatenate([top, bot], axis=1)


SB_LOG_F32_ZERO = -105.0


def _sb_logits(q, kb, mask):
    z = _dot_nt(q, kb) * (HEAD_DIM_C ** -0.5)
    sp = jnp.maximum(z, 0.0) + jnp.log(1.0 + jnp.exp2(jnp.abs(z) * -LOG2_E))
    if mask is not None:
        sp = jnp.where(mask, sp, 0.0)
    return z, sp.astype(BF16)


def _sb_suffix_sums(sp_list, tri):
    rows = sp_list[0].shape[0]
    out = _dot(jnp.concatenate(sp_list, axis=0), tri)
    return [out[t * rows:(t + 1) * rows] for t in range(len(sp_list))]


def _sb_scores(q, kb, tri, mask):
    z, sp = _sb_logits(q, kb, mask)
    return z, _sb_suffix_sums([sp], tri)[0]


def _sb_weighted(z, incl, carry, vb, mask):
    w = jnp.exp(z - (incl + carry))
    if mask is not None:
        w = jnp.where(mask, w, 0.0)
    return _dot(w.astype(BF16), vb)


def _sb_body(q_ref, k_ref, v_ref, tri_prev_ref, tri_half_ref, o_ref):
    qi = pl.program_id(1)
    hd, nh = HEAD_DIM_C, SB_HALF
    heads = range(SB_HEADS_PER_STEP)
    tri_prev = tri_prev_ref[...]
    tri_half = tri_half_ref[...]
    row_id = lax.broadcasted_iota(jnp.int32, (nh, nh), 0)
    col_id = lax.broadcasted_iota(jnp.int32, (nh, nh), 1)
    before = col_id < row_id
    cols = lambda h: slice(h * hd, (h + 1) * hd)
    rows = lambda half: slice(half * nh, (half + 1) * nh)

    def kv(start, size, h):
        start = pl.multiple_of(start, nh)
        return k_ref[pl.ds(start, size), cols(h)], v_ref[pl.ds(start, size), cols(h)]

    @pl.when(qi == 0)
    def _first_block():
        for h in heads:
            k0, v0 = kv(0, nh, h)
            k1, v1 = kv(nh, nh, h)
            z, incl = _sb_scores(q_ref[rows(0), cols(h)], k0, tri_half, before)
            o_ref[rows(0), cols(h)] = _sb_weighted(z, incl, 0.0, v0, before).astype(o_ref.dtype)
            q_bot = q_ref[rows(1), cols(h)]
            zd, incl_d = _sb_scores(q_bot, k1, tri_half, before)
            zp, incl_p = _sb_scores(q_bot, k0, tri_half, None)
            acc = (_sb_weighted(zd, incl_d, 0.0, v1, before)
                   + _sb_weighted(zp, incl_p, incl_d[:, 0:1], v0, None))
            o_ref[rows(1), cols(h)] = acc.astype(o_ref.dtype)

    @pl.when(qi > 0)
    def _later_blocks():
        base = qi * SB_TQ
        tiles = [(h, half) for h in heads for half in range(SB_TQ // nh)]
        window_start = lambda half: base - SB_PREV + half * nh
        qs = [q_ref[rows(half), cols(h)] for h, half in tiles]
        kv_prev = [kv(window_start(half), SB_PREV, h) for h, half in tiles]
        kv_diag = [kv(window_start(half) + SB_PREV, nh, h) for h, half in tiles]
        l_diag = [_sb_logits(qs[t], kv_diag[t][0], before) for t in range(len(tiles))]
        l_prev = [_sb_logits(qs[t], kv_prev[t][0], None) for t in range(len(tiles))]
        i_diag = _sb_suffix_sums([sp for _, sp in l_diag], tri_half)
        i_prev = _sb_suffix_sums([sp for _, sp in l_prev], tri_prev)
        state = ()
        for t in range(len(tiles)):
            zd, incl_d, zp, incl_p = l_diag[t][0], i_diag[t], l_prev[t][0], i_prev[t]
            carry = incl_d[:, 0:1]
            acc = (_sb_weighted(zd, incl_d, 0.0, kv_diag[t][1], before)
                   + _sb_weighted(zp, incl_p, carry, kv_prev[t][1], None))
            state += (carry + incl_p[:, 0:1], acc)

        def live(loop_state):
            n, state = loop_state
            stick = jnp.min(functools.reduce(jnp.minimum, state[0::2]))
            more_keys = window_start(1) - (n + 1) * nh >= 0
            return jnp.logical_and(more_keys, -stick > SB_LOG_F32_ZERO)

        def body(loop_state):
            n, state = loop_state
            starts = [window_start(half) - (n + 1) * nh for _, half in tiles]
            kvs = [kv(jnp.maximum(starts[t], 0), nh, h) for t, (h, _) in enumerate(tiles)]
            logits = [_sb_logits(qs[t], kvs[t][0], None) for t in range(len(tiles))]
            incls = _sb_suffix_sums([sp for _, sp in logits], tri_half)
            out = ()
            for t in range(len(tiles)):
                valid = starts[t] >= 0
                carry, acc = state[2 * t], state[2 * t + 1]
                add = _sb_weighted(logits[t][0], incls[t], carry, kvs[t][1], None)
                out += (carry + jnp.where(valid, incls[t][:, 0:1], 0.0),
                        acc + jnp.where(valid, add, 0.0))
            return n + 1, out

        _, state = lax.while_loop(live, body, (jnp.int32(0), state))
        for t, (h, half) in enumerate(tiles):
            o_ref[rows(half), cols(h)] = state[2 * t + 1].astype(o_ref.dtype)


def _suffix_sum_matrix(n):
    jj = jnp.arange(n)
    return (jj[:, None] >= jj[None, :]).astype(BF16)


def _stick_breaking(proj1):
    s = proj1.shape[0]
    hw = SB_HEADS_PER_STEP * HEAD_DIM_C
    n_groups = N_HEADS_C // SB_HEADS_PER_STEP
    whole = lambda shape: pl.BlockSpec(shape, lambda g, i: (0,) * len(shape))
    return pl.pallas_call(
        _sb_body,
        grid=(n_groups, s // SB_TQ),
        in_specs=[pl.BlockSpec((SB_TQ, hw), lambda g, i: (i, g)),
                  pl.BlockSpec((s, hw), lambda g, i: (0, n_groups + g)),
                  pl.BlockSpec((s, hw), lambda g, i: (0, 2 * n_groups + g)),
                  whole((SB_PREV, SB_PREV)), whole((SB_HALF, SB_HALF))],
        out_specs=pl.BlockSpec((SB_TQ, hw), lambda g, i: (i, g)),
        out_shape=jax.ShapeDtypeStruct((s, N_HEADS_C * HEAD_DIM_C), BF16),
        compiler_params=_params(2),
        name="stick_breaking",
    )(proj1, proj1, proj1, _suffix_sum_matrix(SB_PREV), _suffix_sum_matrix(SB_HALF))


def kernel(x, rel_bias, norm_mix, even_w_in, even_conv_w, even_conv_b, even_gate_a_w,
           even_gate_a_b, even_gate_x_w, even_gate_x_b, even_lru_lambda, even_sinks,
           even_w_out, odd_w_in, odd_w_out, norm_mlp, w_up, w_down, final_norm):
    b, s, d = x.shape
    h = x.reshape(b * s, d)

    proj0 = _norm_matmul(h, norm_mix[0], even_w_in[0], F32)
    y_a = _swa(proj0, rel_bias, even_sinks[0], _swa_bucket_table())
    y_b = _lru(proj0, even_conv_w[0], even_conv_b[0],
               _pair_block_diag(even_gate_a_w[0]).astype(BF16), even_gate_a_b[0],
               _pair_block_diag(even_gate_x_w[0]).astype(BF16), even_gate_x_b[0],
               even_lru_lambda[0])
    h = _matmul_res([y_a, y_b], even_w_out[0], h)
    h = _mlp(h, norm_mlp[0], w_up, w_down, 0, None)

    proj1 = _norm_matmul(h, norm_mix[1], odd_w_in[0], BF16)
    y = _stick_breaking(proj1)
    h = _matmul_res([y], odd_w_out[0], h)
    h = _mlp(h, norm_mlp[1], w_up, w_down, 1, final_norm)
    return h.reshape(b, s, d)
```
